```python
import jax, jax.numpy as jnp
from jax import lax
import numpy as np

D_MODEL = 1024
BATCH = 8
SEQ = 4096
DEPTH = 2
DEC_BATCH = 128
DEC_SEQ = 8
PAST_LEN = 16384
PAGE_SIZE = 128

CONV_CH = 512
CONV_WIDTH = 31
SB_HEADS = 8
SB_KV_HEADS = 2
SB_GROUP = SB_HEADS // SB_KV_HEADS
SB_HEAD_DIM = 64
SB_WIDTH = SB_HEADS * SB_HEAD_DIM
SB_KV_WIDTH = SB_KV_HEADS * SB_HEAD_DIM
MLA_HEADS = 8
MLA_Q_RANK = 256
MLA_KV_RANK = 256
MLA_NOPE_DIM = 64
MLA_ROPE_DIM = 32
MLA_V_DIM = 64
MLA_WIDTH = MLA_HEADS * MLA_V_DIM
MLA_SCALE = (MLA_NOPE_DIM + MLA_ROPE_DIM) ** -0.5
ROPE_THETA = 10000.0
N_BRANCHES = 3
Q_BLOCK = 128
NORM_EPS = 1e-6
IN_SIZES = (CONV_CH, CONV_CH, CONV_CH, SB_WIDTH, SB_KV_WIDTH, SB_KV_WIDTH, SB_WIDTH, MLA_Q_RANK, MLA_KV_RANK, MLA_ROPE_DIM, MLA_WIDTH, N_BRANCHES * D_MODEL)
D_IN = sum(IN_SIZES)

kernel_name = "hybrid_conv_stickbreak_mla_decoder_step"


def in_offsets():
    return np.cumsum(np.array(IN_SIZES))[:-1].tolist()


def rmsnorm(x, g):
    xf = x.astype(jnp.float32)
    y = xf * lax.rsqrt(jnp.mean(xf * xf, axis=-1, keepdims=True) + NORM_EPS)
    return (y * g.astype(jnp.float32)).astype(x.dtype)


def layernorm(x, g, b):
    xf = x.astype(jnp.float32)
    mu = jnp.mean(xf, axis=-1, keepdims=True)
    xc = xf - mu
    y = xc * lax.rsqrt(jnp.mean(xc * xc, axis=-1, keepdims=True) + NORM_EPS)
    return (y * g.astype(jnp.float32) + b.astype(jnp.float32)).astype(x.dtype)


def rope_angles(pos):
    inv = ROPE_THETA ** (-jnp.arange(0, MLA_ROPE_DIM, 2, dtype=jnp.float32) / MLA_ROPE_DIM)
    ang = pos.astype(jnp.float32)[:, None] * inv[None, :]
    return jnp.cos(ang), jnp.sin(ang)


def apply_rope(x, cos, sin):
    x1, x2 = jnp.split(x, 2, axis=-1)
    cos = cos.astype(x.dtype)
    sin = sin.astype(x.dtype)
    return jnp.concatenate([x1 * cos - x2 * sin, x1 * sin + x2 * cos], axis=-1)


def causal_depthwise_conv(h, prefix, w, b):
    xp = jnp.concatenate([prefix.astype(h.dtype), h], axis=1)
    y = lax.conv_general_dilated(xp, w[:, None, :].astype(h.dtype), window_strides=(1,), padding='VALID',
                                 dimension_numbers=('NWC', 'WIO', 'NWC'), feature_group_count=h.shape[-1])
    return y + b, xp[:, -(CONV_WIDTH - 1):]


def stick_breaking(q, k, v, q_pos, k_pos):
    z = jnp.einsum('bqhgd,bshd->bhgqs', q, k).astype(jnp.float32) * (SB_HEAD_DIM ** -0.5)
    mask = k_pos[None, :] < q_pos[:, None]
    log_beta = jax.nn.log_sigmoid(z)
    log_1m = jnp.where(mask, jax.nn.log_sigmoid(-z), 0.0)
    suffix = lax.cumsum(log_1m, axis=z.ndim - 1, reverse=True) - log_1m
    weights = jnp.where(mask, jnp.exp(log_beta + suffix), 0.0)
    return jnp.einsum('bhgqs,bshd->bqhgd', weights.astype(v.dtype), v)


def mla_attend(q_lat, q_rope, c, kr, q_pos, k_pos):
    s = (jnp.einsum('bqhc,bsc->bhqs', q_lat, c) + jnp.einsum('bqhr,bsr->bhqs', q_rope, kr)).astype(jnp.float32) * MLA_SCALE
    s = jnp.where(k_pos[None, :] <= q_pos[:, None], s, -jnp.inf)
    p = jax.nn.softmax(s, axis=-1)
    return jnp.einsum('bhqs,bsc->bqhc', p.astype(c.dtype), c)


def hybrid_layer(x, pos, conv_prefix, attend, norm_g, w_in, conv_w, conv_b, conv_ln_g, conv_ln_b,
                 q_norm_g, w_uq, kv_norm_g, w_uk, w_uv, w_out_conv, w_out_sb, w_out_mla, w_o):
    b, t, _ = x.shape
    u = rmsnorm(x, norm_g)
    (a, a_gate, z_conv, q_sb, k_sb, v_sb, z_sb, c_q, c_kv, k_r, z_mla, gate_logits) = jnp.split(u @ w_in, in_offsets(), axis=-1)
    h = a * jax.nn.sigmoid(a_gate)
    y_conv, conv_state = causal_depthwise_conv(h, conv_prefix, conv_w, conv_b)
    br_conv = jax.nn.silu(layernorm(y_conv, conv_ln_g, conv_ln_b)) * jax.nn.silu(z_conv)
    q_sb = q_sb.reshape(b, t, SB_KV_HEADS, SB_GROUP, SB_HEAD_DIM)
    k_sb = k_sb.reshape(b, t, SB_KV_HEADS, SB_HEAD_DIM)
    v_sb = v_sb.reshape(b, t, SB_KV_HEADS, SB_HEAD_DIM)
    cos, sin = rope_angles(pos)
    q = (rmsnorm(c_q, q_norm_g) @ w_uq).reshape(b, t, MLA_HEADS, MLA_NOPE_DIM + MLA_ROPE_DIM)
    q_nope = q[..., :MLA_NOPE_DIM]
    q_rope = apply_rope(q[..., MLA_NOPE_DIM:], cos[:, None], sin[:, None])
    c_kv = rmsnorm(c_kv, kv_norm_g)
    k_r = apply_rope(k_r, cos, sin)
    q_lat = jnp.einsum('bthn,chn->bthc', q_nope, w_uk)
    o_sb, o_lat = attend(q_sb, k_sb, v_sb, q_lat, q_rope, c_kv, k_r)
    o_mla = jnp.einsum('bthc,chv->bthv', o_lat, w_uv)
    br_sb = o_sb.reshape(b, t, SB_WIDTH) * jax.nn.silu(z_sb)
    br_mla = o_mla.reshape(b, t, MLA_WIDTH) * jax.nn.silu(z_mla)
    g = jax.nn.sigmoid(gate_logits).reshape(b, t, N_BRANCHES, D_MODEL)
    merged = (g[:, :, 0] * (br_conv @ w_out_conv) + g[:, :, 1] * (br_sb @ w_out_sb)
              + g[:, :, 2] * (br_mla @ w_out_mla))
    return x + merged @ w_o, (k_sb, v_sb, c_kv, k_r, conv_state)


def setup_inputs(seed: int = 0) -> dict:
    key = jax.random.key(seed)
    ks = jax.random.split(key, 32)
    f32 = jnp.float32
    n_pages = PAST_LEN // PAGE_SIZE
    n_used = DEC_BATCH * n_pages
    n_pool = (n_used * 5) // 4

    def nrm(k, shape, scale=1.0):
        return jax.random.normal(k, shape, f32) * scale

    page_table = jax.random.permutation(ks[0], n_pool)[:n_used].reshape(DEC_BATCH, n_pages).astype(jnp.int32)
    return {
        'x_prompt': nrm(ks[1], (BATCH, SEQ, D_MODEL)),
        'x_sample': nrm(ks[2], (DEC_BATCH, DEC_SEQ, D_MODEL)),
        'cache_sb_k': nrm(ks[3], (DEPTH, n_pool, PAGE_SIZE, SB_KV_HEADS, SB_HEAD_DIM)),
        'cache_sb_v': nrm(ks[4], (DEPTH, n_pool, PAGE_SIZE, SB_KV_HEADS, SB_HEAD_DIM)),
        'cache_mla_latent': nrm(ks[5], (DEPTH, n_pool, PAGE_SIZE, MLA_KV_RANK)),
        'cache_mla_rope': nrm(ks[6], (DEPTH, n_pool, PAGE_SIZE, MLA_ROPE_DIM)),
        'state_conv': nrm(ks[7], (DEPTH, DEC_BATCH, CONV_WIDTH - 1, CONV_CH), 0.5),
        'page_table': page_table,
        'norm_g': 1.0 + nrm(ks[8], (DEPTH, D_MODEL), 0.01),
        'w_in': nrm(ks[9], (DEPTH, D_MODEL, D_IN), D_MODEL ** -0.5),
        'conv_w': nrm(ks[10], (DEPTH, CONV_WIDTH, CONV_CH), CONV_WIDTH ** -0.5),
        'conv_b': nrm(ks[11], (DEPTH, CONV_CH), 0.01),
        'conv_ln_g': 1.0 + nrm(ks[12], (DEPTH, CONV_CH), 0.01),
        'conv_ln_b': nrm(ks[13], (DEPTH, CONV_CH), 0.01),
        'mla_q_norm_g': 1.0 + nrm(ks[14], (DEPTH, MLA_Q_RANK), 0.01),
        'mla_w_uq': nrm(ks[15], (DEPTH, MLA_Q_RANK, MLA_HEADS * (MLA_NOPE_DIM + MLA_ROPE_DIM)), MLA_Q_RANK ** -0.5),
        'mla_kv_norm_g': 1.0 + nrm(ks[16], (DEPTH, MLA_KV_RANK), 0.01),
        'mla_w_uk': nrm(ks[17], (DEPTH, MLA_KV_RANK, MLA_HEADS, MLA_NOPE_DIM), MLA_KV_RANK ** -0.5),
        'mla_w_uv': nrm(ks[18], (DEPTH, MLA_KV_RANK, MLA_HEADS, MLA_V_DIM), MLA_KV_RANK ** -0.5),
        'w_out_conv': nrm(ks[19], (DEPTH, CONV_CH, D_MODEL), CONV_CH ** -0.5),
        'w_out_sb': nrm(ks[20], (DEPTH, SB_WIDTH, D_MODEL), SB_WIDTH ** -0.5),
        'w_out_mla': nrm(ks[21], (DEPTH, MLA_WIDTH, D_MODEL), MLA_WIDTH ** -0.5),
        'w_o': nrm(ks[22], (DEPTH, D_MODEL, D_MODEL), D_MODEL ** -0.5),
        'final_norm_g': 1.0 + nrm(ks[23], (D_MODEL,), 0.01),
    }


def reference(x_prompt, x_sample, cache_sb_k, cache_sb_v, cache_mla_latent, cache_mla_rope, state_conv, page_table,
              norm_g, w_in, conv_w, conv_b, conv_ln_g, conv_ln_b, mla_q_norm_g, mla_w_uq, mla_kv_norm_g, mla_w_uk,
              mla_w_uv, w_out_conv, w_out_sb, w_out_mla, w_o, final_norm_g):
    b_p, seq, _ = x_prompt.shape
    t_new = x_sample.shape[1]
    past_len = page_table.shape[1] * PAGE_SIZE
    pos_p = jnp.arange(seq, dtype=jnp.int32)
    pos_s = past_len + jnp.arange(t_new, dtype=jnp.int32)
    k_pos_s = jnp.arange(past_len + t_new, dtype=jnp.int32)
    n_blocks = seq // Q_BLOCK

    def prompt_attend(q_sb, k_sb, v_sb, q_lat, q_rope, c, kr):
        def blk(a):
            return jnp.swapaxes(a.reshape(b_p, n_blocks, Q_BLOCK, *a.shape[2:]), 0, 1)

        def unblk(a):
            return jnp.swapaxes(a, 0, 1).reshape(b_p, seq, *a.shape[3:])

        def one_block(args):
            qs, ql, qr, qp = args
            return (stick_breaking(qs, k_sb, v_sb, qp, pos_p), mla_attend(ql, qr, c, kr, qp, pos_p))

        o_sb, o_lat = lax.map(one_block, (blk(q_sb), blk(q_lat), blk(q_rope), pos_p.reshape(n_blocks, Q_BLOCK)))
        return unblk(o_sb), unblk(o_lat)

    def make_sample_attend(l):
        def attend(q_sb, k_sb, v_sb, q_lat, q_rope, c, kr):
            def one_seq(args):
                pt, qs, kn, vn, ql, qr, cn, krn = args
                k_all = jnp.concatenate([cache_sb_k[l, pt].reshape(-1, SB_KV_HEADS, SB_HEAD_DIM), kn], axis=0)
                v_all = jnp.concatenate([cache_sb_v[l, pt].reshape(-1, SB_KV_HEADS, SB_HEAD_DIM), vn], axis=0)
                c_all = jnp.concatenate([cache_mla_latent[l, pt].reshape(-1, MLA_KV_RANK), cn], axis=0)
                kr_all = jnp.concatenate([cache_mla_rope[l, pt].reshape(-1, MLA_ROPE_DIM), krn], axis=0)
                o_sb = stick_breaking(qs[None], k_all[None], v_all[None], pos_s, k_pos_s)[0]
                o_lat = mla_attend(ql[None], qr[None], c_all[None], kr_all[None], pos_s, k_pos_s)[0]
                return o_sb, o_lat

            return lax.map(one_seq, (page_table, q_sb, k_sb, v_sb, q_lat, q_rope, c, kr))
        return attend

    xp, xs = x_prompt, x_sample
    st_p, st_s = [], []
    for l in range(DEPTH):
        params = (norm_g[l], w_in[l], conv_w[l], conv_b[l], conv_ln_g[l], conv_ln_b[l], mla_q_norm_g[l], mla_w_uq[l],
                  mla_kv_norm_g[l], mla_w_uk[l], mla_w_uv[l], w_out_conv[l], w_out_sb[l], w_out_mla[l], w_o[l])
        zero_prefix = jnp.zeros((b_p, CONV_WIDTH - 1, CONV_CH), xp.dtype)
        xp, sp = hybrid_layer(xp, pos_p, zero_prefix, prompt_attend, *params)
        xs, ss = hybrid_layer(xs, pos_s, state_conv[l], make_sample_attend(l), *params)
        st_p.append(sp)
        st_s.append(ss)

    y_prompt = rmsnorm(xp, final_norm_g)
    y_sample = rmsnorm(xs, final_norm_g)
    new_sb_k_prompt = jnp.stack([s[0] for s in st_p])
    new_sb_v_prompt = jnp.stack([s[1] for s in st_p])
    new_latent_prompt = jnp.stack([s[2] for s in st_p])
    new_rope_prompt = jnp.stack([s[3] for s in st_p])
    new_conv_prompt = jnp.stack([s[4] for s in st_p])
    new_sb_k_sample = jnp.stack([s[0] for s in st_s])
    new_sb_v_sample = jnp.stack([s[1] for s in st_s])
    new_latent_sample = jnp.stack([s[2] for s in st_s])
    new_rope_sample = jnp.stack([s[3] for s in st_s])
    new_conv_sample = jnp.stack([s[4] for s in st_s])
    return (y_prompt, y_sample, new_sb_k_prompt, new_sb_v_prompt, new_latent_prompt, new_rope_prompt, new_conv_prompt,
            new_sb_k_sample, new_sb_v_sample, new_latent_sample, new_rope_sample, new_conv_sample)
```

```python
import functools

import numpy as np
import jax
import jax.numpy as jnp
from jax import lax
from jax.experimental import pallas as pl
from jax.experimental.pallas import tpu as pltpu

F32 = jnp.float32
BF16 = jnp.bfloat16

D_MODEL = 1024
PAGE_SIZE = 128
CONV_CH = 512
CONV_WIDTH = 31
SB_HEADS = 8
SB_KV_HEADS = 2
SB_GROUP = SB_HEADS // SB_KV_HEADS
SB_HEAD_DIM = 64
SB_WIDTH = SB_HEADS * SB_HEAD_DIM
SB_KV_WIDTH = SB_KV_HEADS * SB_HEAD_DIM
MLA_HEADS = 8
MLA_Q_RANK = 256
MLA_KV_RANK = 256
MLA_NOPE_DIM = 64
MLA_ROPE_DIM = 32
MLA_V_DIM = 64
MLA_WIDTH = MLA_HEADS * MLA_V_DIM
MLA_SCALE = (MLA_NOPE_DIM + MLA_ROPE_DIM) ** -0.5
ROPE_THETA = 10000.0
N_BRANCHES = 3
NORM_EPS = 1e-6
IN_SIZES = (CONV_CH, CONV_CH, CONV_CH, SB_WIDTH, SB_KV_WIDTH, SB_KV_WIDTH, SB_WIDTH, MLA_Q_RANK, MLA_KV_RANK,
            MLA_ROPE_DIM, MLA_WIDTH, N_BRANCHES * D_MODEL)
IN_OFF = tuple(int(v) for v in np.concatenate([[0], np.cumsum(IN_SIZES)]))

LANES = 128
CONV_HALO = 32
LOG2E = 1.4426950408889634
MLA_EXP2_SCALE = MLA_SCALE * LOG2E
NEG_BIG = -1e30
VMEM_LIMIT = 48 * 1024 * 1024
SB_HEAD_ORDER = tuple(h for p in range(SB_GROUP) for h in (p, SB_GROUP + p))


def _dot(a, b):
    return jnp.dot(a, b, preferred_element_type=F32)


def _dot_nt(a, b):
    return lax.dot_general(a, b, (((1,), (1,)), ((), ())), preferred_element_type=F32)


def _sigmoid(x):
    return 1.0 / (1.0 + jnp.exp(-x))


def _silu(x):
    return x * _sigmoid(x)


def _rms(x, g):
    return x * lax.rsqrt(jnp.mean(x * x, axis=-1, keepdims=True) + NORM_EPS) * g


def _rep_lanes(x, width):
    reps = width // LANES
    return x if reps == 1 else jnp.concatenate([x] * reps, axis=1)


def _cparams(sem):
    return pltpu.CompilerParams(dimension_semantics=sem, vmem_limit_bytes=VMEM_LIMIT)


def _pre_kernel(x_ref, cos_ref, sin_ref, ng_ref, wa_ref, wsb_ref, wmla_ref, qg_ref, wuq_ref, kvg_ref, wuk_ref,
                h_ref, qsb_ref, k_ref, v_ref, qlat_ref, qrope_ref, ckv_ref, kr_ref):
    ub = _rms(x_ref[...], ng_ref[...]).astype(BF16)
    ag = _dot(ub, wa_ref[...])
    h_ref[...] = ag[:, :CONV_CH] * _sigmoid(ag[:, CONV_CH:])
    sb = _dot(ub, wsb_ref[...])
    for hd in range(SB_HEADS):
        qsb_ref[hd] = (sb[:, LANES * hd:LANES * (hd + 1)] * (SB_HEAD_DIM ** -0.5)).astype(BF16)
    k_ref[...] = sb[:, SB_HEADS * LANES:SB_HEADS * LANES + SB_KV_WIDTH]
    v_ref[...] = sb[:, SB_HEADS * LANES + SB_KV_WIDTH:]
    m = _dot(ub, wmla_ref[...])
    cos = cos_ref[...]
    sin = sin_ref[...]
    ckv_ref[...] = _rms(m[:, MLA_Q_RANK:MLA_Q_RANK + MLA_KV_RANK], kvg_ref[...])
    o1 = MLA_Q_RANK + MLA_KV_RANK
    kr_ref[...] = (m[:, o1:o1 + MLA_ROPE_DIM] * cos[:, :MLA_ROPE_DIM]
                   + m[:, o1 + LANES:o1 + LANES + MLA_ROPE_DIM] * sin[:, :MLA_ROPE_DIM])
    cqn = _rms(m[:, :MLA_Q_RANK], qg_ref[...]).astype(BF16)
    qq = _dot(cqn, wuq_ref[...])
    nw = MLA_HEADS * MLA_NOPE_DIM
    rw = MLA_HEADS * MLA_ROPE_DIM
    rot = qq[:, nw:nw + rw] * cos + qq[:, nw + rw:] * sin
    for hd in range(MLA_HEADS):
        qrope_ref[hd] = rot[:, MLA_ROPE_DIM * hd:MLA_ROPE_DIM * (hd + 1)].astype(BF16)
        pair = hd // 2
        qn = qq[:, LANES * pair:LANES * (pair + 1)].astype(BF16)
        qlat_ref[hd] = _dot(qn, wuk_ref[hd]).astype(BF16)


def _pre_call(x, cos, sin, lw, tm):
    n = x.shape[0]
    nblk = cos.shape[0] // tm
    row = lambda i: (i, 0)
    hrow = lambda i: (0, i, 0)
    c2 = lambda i: (0, 0)
    c3 = lambda i: (0, 0, 0)
    tab = lambda i: (i % nblk, 0)
    rw = MLA_HEADS * MLA_ROPE_DIM
    in_specs = [
        pl.BlockSpec((tm, D_MODEL), row),
        pl.BlockSpec((tm, rw), tab),
        pl.BlockSpec((tm, rw), tab),
        pl.BlockSpec((1, D_MODEL), c2),
        pl.BlockSpec(lw['wa'].shape, c2),
        pl.BlockSpec(lw['wsb'].shape, c2),
        pl.BlockSpec(lw['wmla'].shape, c2),
        pl.BlockSpec((1, MLA_Q_RANK), c2),
        pl.BlockSpec(lw['wuq'].shape, c2),
        pl.BlockSpec((1, MLA_KV_RANK), c2),
        pl.BlockSpec(lw['wuk'].shape, c3),
    ]
    out_shape = [
        jax.ShapeDtypeStruct((n, CONV_CH), F32),
        jax.ShapeDtypeStruct((SB_HEADS, n, LANES), BF16),
        jax.ShapeDtypeStruct((n, SB_KV_WIDTH), F32),
        jax.ShapeDtypeStruct((n, SB_KV_WIDTH), F32),
        jax.ShapeDtypeStruct((MLA_HEADS, n, MLA_KV_RANK), BF16),
        jax.ShapeDtypeStruct((MLA_HEADS, n, MLA_ROPE_DIM), BF16),
        jax.ShapeDtypeStruct((n, MLA_KV_RANK), F32),
        jax.ShapeDtypeStruct((n, MLA_ROPE_DIM), F32),
    ]
    out_specs = [
        pl.BlockSpec((tm, CONV_CH), row),
        pl.BlockSpec((SB_HEADS, tm, LANES), hrow),
        pl.BlockSpec((tm, SB_KV_WIDTH), row),
        pl.BlockSpec((tm, SB_KV_WIDTH), row),
        pl.BlockSpec((MLA_HEADS, tm, MLA_KV_RANK), hrow),
        pl.BlockSpec((MLA_HEADS, tm, MLA_ROPE_DIM), hrow),
        pl.BlockSpec((tm, MLA_KV_RANK), row),
        pl.BlockSpec((tm, MLA_ROPE_DIM), row),
    ]
    return pl.pallas_call(
        _pre_kernel, grid=(n // tm,), in_specs=in_specs, out_specs=out_specs, out_shape=out_shape,
        compiler_params=_cparams(("parallel",)), name="pre",
    )(x, cos, sin, lw['norm_g'], lw['wa'], lw['wsb'], lw['wmla'], lw['qg'], lw['wuq'], lw['kvg'], lw['wuk'])


def _conv_kernel(h_ref, pre_ref, w_ref, b_ref, g_ref, beta_ref, o_ref, win_ref, y_ref, *, tt, rc):
    t = pl.program_id(1)

    @pl.when(t == 0)
    def _():
        win_ref[0:CONV_HALO, :] = pre_ref[0]

    win_ref[CONV_HALO:CONV_HALO + tt, :] = h_ref[0]
    shift = CONV_HALO - (CONV_WIDTH - 1)
    for r in range(tt // rc):
        for c in range(CONV_CH // LANES):
            cs = slice(c * LANES, (c + 1) * LANES)
            acc = jnp.broadcast_to(b_ref[:, cs], (rc, LANES))
            for k in range(CONV_WIDTH):
                lo = r * rc + shift + k
                acc = acc + w_ref[k:k + 1, cs] * win_ref[lo:lo + rc, cs]
            y_ref[r * rc:(r + 1) * rc, cs] = acc
    y = y_ref[...]
    mu = jnp.mean(y, axis=-1, keepdims=True)
    yc = y - mu
    ln = yc * lax.rsqrt(jnp.mean(yc * yc, axis=-1, keepdims=True) + NORM_EPS) * g_ref[...] + beta_ref[...]
    o_ref[0] = _silu(ln)
    win_ref[0:CONV_HALO, :] = win_ref[tt:tt + CONV_HALO, :]


def _conv_call(h, prefix, lw):
    b, t, _ = h.shape
    tt = min(256, t)
    rc = min(32, tt)
    c2 = lambda i, j: (0, 0)
    kern = functools.partial(_conv_kernel, tt=tt, rc=rc)
    return pl.pallas_call(
        kern, grid=(b, t // tt),
        in_specs=[
            pl.BlockSpec((1, tt, CONV_CH), lambda i, j: (i, j, 0)),
            pl.BlockSpec((1, CONV_HALO, CONV_CH), lambda i, j: (i, 0, 0)),
            pl.BlockSpec((CONV_WIDTH, CONV_CH), c2),
            pl.BlockSpec((1, CONV_CH), c2),
            pl.BlockSpec((1, CONV_CH), c2),
            pl.BlockSpec((1, CONV_CH), c2),
        ],
        out_specs=pl.BlockSpec((1, tt, CONV_CH), lambda i, j: (i, j, 0)),
        out_shape=jax.ShapeDtypeStruct((b, t, CONV_CH), F32),
        scratch_shapes=[pltpu.VMEM((tt + CONV_HALO, CONV_CH), F32), pltpu.VMEM((tt, CONV_CH), F32)],
        compiler_params=_cparams(("parallel", "arbitrary")), name="conv",
    )(h, prefix, lw['conv_w'], lw['conv_b'], lw['conv_ln_g'], lw['conv_ln_b'])


def _sb_tile(z, tri, run, mask):
    lb = jnp.minimum(z, 0.0) - jnp.log1p(jnp.exp(-jnp.abs(z)))
    l1m = lb - z
    if mask is not None:
        l1m = jnp.where(mask, l1m, 0.0)
    hi = l1m.astype(BF16)
    lo = (l1m - hi.astype(F32)).astype(BF16)
    suf = _dot(hi, tri) + _dot(lo, tri)
    w = jnp.exp(lb + suf + run)
    if mask is not None:
        w = jnp.where(mask, w, 0.0)
    return w, jnp.sum(l1m, axis=1, keepdims=True)


def _tri_matrix(tk):
    j = np.arange(tk)[:, None]
    s = np.arange(tk)[None, :]
    return jnp.asarray((j > s).astype(np.float32), dtype=BF16)


def _sb_pair_store(o_ref, acc0, acc1, rows):
    lane = lax.broadcasted_iota(jnp.int32, (rows, LANES), 1)
    for p in range(SB_GROUP):
        o_ref[:, LANES * p:LANES * (p + 1)] = jnp.where(lane < SB_HEAD_DIM, acc0[p * rows:(p + 1) * rows],
                                                         acc1[p * rows:(p + 1) * rows])


def _sbp_kernel(qi_ref, kb_ref, first_ref, last_ref, q_ref, k_ref, v_ref, tri_ref, o_ref, acc_ref, run_ref, *, tq):
    step = pl.program_id(1)
    rows = SB_GROUP * tq

    @pl.when(first_ref[step] == 1)
    def _():
        acc_ref[...] = jnp.zeros_like(acc_ref)
        run_ref[...] = jnp.zeros_like(run_ref)

    def body(masked):
        kb = k_ref[...].astype(BF16)
        vb = v_ref[...].astype(BF16)
        tri = tri_ref[...]
        mask = None
        if masked:
            tpos = lax.broadcasted_iota(jnp.int32, (SB_GROUP, tq, tq), 1).reshape(rows, tq)
            spos = lax.broadcasted_iota(jnp.int32, (rows, tq), 1)
            mask = spos < tpos
        for g in range(SB_KV_HEADS):
            q = q_ref[SB_GROUP * g:SB_GROUP * (g + 1)].reshape(rows, LANES)
            z = _dot_nt(q, kb)
            run = run_ref[g]
            w, tot = _sb_tile(z, tri, _rep_lanes(run, tq), mask)
            acc_ref[g] += _dot(w.astype(BF16), vb)
            run_ref[g] = run + tot

    diag = qi_ref[step] == kb_ref[step]

    @pl.when(diag)
    def _():
        body(True)

    @pl.when(jnp.logical_not(diag))
    def _():
        body(False)

    @pl.when(last_ref[step] == 1)
    def _():
        _sb_pair_store(o_ref, acc_ref[0], acc_ref[1], tq)


def _sbp_call(qsb, k, v, b, t):
    tq = min(256, t)
    nq = t // tq
    qi_l, kb_l, first_l, last_l = [], [], [], []
    for qi in range(nq):
        for kb in range(qi, -1, -1):
            qi_l.append(qi)
            kb_l.append(kb)
            first_l.append(int(kb == qi))
            last_l.append(int(kb == 0))
    tabs = [jnp.asarray(np.array(a, np.int32)) for a in (qi_l, kb_l, first_l, last_l)]
    n = b * t
    grid_spec = pltpu.PrefetchScalarGridSpec(
        num_scalar_prefetch=4, grid=(b, len(qi_l)),
        in_specs=[
            pl.BlockSpec((SB_HEADS, tq, LANES), lambda bi, s, qt, kt, ft, lt: (0, bi * nq + qt[s], 0)),
            pl.BlockSpec((tq, SB_KV_WIDTH), lambda bi, s, qt, kt, ft, lt: (bi * nq + kt[s], 0)),
            pl.BlockSpec((tq, SB_KV_WIDTH), lambda bi, s, qt, kt, ft, lt: (bi * nq + kt[s], 0)),
            pl.BlockSpec((tq, tq), lambda bi, s, qt, kt, ft, lt: (0, 0)),
        ],
        out_specs=pl.BlockSpec((tq, SB_WIDTH), lambda bi, s, qt, kt, ft, lt: (bi * nq + qt[s], 0)),
        scratch_shapes=[pltpu.VMEM((SB_KV_HEADS, SB_GROUP * tq, LANES), F32),
                        pltpu.VMEM((SB_KV_HEADS, SB_GROUP * tq, LANES), F32)],
    )
    return pl.pallas_call(
        functools.partial(_sbp_kernel, tq=tq), grid_spec=grid_spec,
        out_shape=jax.ShapeDtypeStruct((n, SB_WIDTH), F32),
        compiler_params=_cparams(("parallel", "arbitrary")), name="sbp",
    )(*tabs, qsb, k, v, _tri_matrix(tq))


def _mla_update(s, cb, m_ref, l_ref, acc_ref):
    tk = s.shape[1]
    m_prev = m_ref[...]
    m_new = jnp.maximum(m_prev, jnp.max(s, axis=1, keepdims=True))
    p = jnp.exp2((s - _rep_lanes(m_new, tk)) * MLA_EXP2_SCALE)
    alpha = jnp.exp2((m_prev - m_new) * MLA_EXP2_SCALE)
    l_ref[...] = alpha * l_ref[...] + jnp.sum(p, axis=1, keepdims=True)
    acc_ref[...] = _rep_lanes(alpha, acc_ref.shape[1]) * acc_ref[...] + _dot(p.astype(BF16), cb)
    m_ref[...] = m_new


def _mla_finish(o_ref, wuv_ref, l_ref, acc_ref, rows):
    inv = _rep_lanes(1.0 / l_ref[...], acc_ref.shape[1])
    o = None
    for hd in range(MLA_HEADS):
        rs = slice(hd * rows, (hd + 1) * rows)
        oh = _dot((acc_ref[rs] * inv[rs]).astype(BF16), wuv_ref[hd])
        o = oh if o is None else o + oh
    o_ref[...] = o


def _mlap_kernel(qi_ref, kb_ref, first_ref, last_ref, ql_ref, qr_ref, c_ref, kr_ref, wuv_ref, o_ref,
                 m_ref, l_ref, acc_ref, *, tq, tk):
    step = pl.program_id(1)
    rows = MLA_HEADS * tq
    qi = qi_ref[step]
    kb = kb_ref[step]

    @pl.when(first_ref[step] == 1)
    def _():
        m_ref[...] = jnp.full_like(m_ref, NEG_BIG)
        l_ref[...] = jnp.zeros_like(l_ref)
        acc_ref[...] = jnp.zeros_like(acc_ref)

    def body(masked):
        cb = c_ref[...].astype(BF16)
        krb = kr_ref[...].astype(BF16)
        ql = ql_ref[...].reshape(rows, MLA_KV_RANK)
        qr = qr_ref[...].reshape(rows, MLA_ROPE_DIM)
        s = _dot_nt(ql, cb) + _dot_nt(qr, krb)
        if masked:
            tpos = qi * tq + lax.broadcasted_iota(jnp.int32, (MLA_HEADS, tq, tk), 1).reshape(rows, tk)
            spos = kb * tk + lax.broadcasted_iota(jnp.int32, (rows, tk), 1)
            s = jnp.where(spos <= tpos, s, NEG_BIG)
        _mla_update(s, cb, m_ref, l_ref, acc_ref)

    straddle = (kb + 1) * tk - 1 > qi * tq

    @pl.when(straddle)
    def _():
        body(True)

    @pl.when(jnp.logical_not(straddle))
    def _():
        body(False)

    @pl.when(last_ref[step] == 1)
    def _():
        _mla_finish(o_ref, wuv_ref, l_ref, acc_ref, tq)


def _mlap_call(qlat, qrope, ckv, kr, wuv, b, t):
    tq = min(128, t)
    tk = min(256, t)
    nq = t // tq
    nk = t // tk
    qi_l, kb_l, first_l, last_l = [], [], [], []
    for qi in range(nq):
        kmax = ((qi + 1) * tq - 1) // tk
        for kb in range(kmax + 1):
            qi_l.append(qi)
            kb_l.append(kb)
            first_l.append(int(kb == 0))
            last_l.append(int(kb == kmax))
    tabs = [jnp.asarray(np.array(a, np.int32)) for a in (qi_l, kb_l, first_l, last_l)]
    n = b * t
    rows = MLA_HEADS * tq
    grid_spec = pltpu.PrefetchScalarGridSpec(
        num_scalar_prefetch=4, grid=(b, len(qi_l)),
        in_specs=[
            pl.BlockSpec((MLA_HEADS, tq, MLA_KV_RANK), lambda bi, s, qt, kt, ft, lt: (0, bi * nq + qt[s], 0)),
            pl.BlockSpec((MLA_HEADS, tq, MLA_ROPE_DIM), lambda bi, s, qt, kt, ft, lt: (0, bi * nq + qt[s], 0)),
            pl.BlockSpec((tk, MLA_KV_RANK), lambda bi, s, qt, kt, ft, lt: (bi * nk + kt[s], 0)),
            pl.BlockSpec((tk, MLA_ROPE_DIM), lambda bi, s, qt, kt, ft, lt: (bi * nk + kt[s], 0)),
            pl.BlockSpec(wuv.shape, lambda bi, s, qt, kt, ft, lt: (0, 0, 0)),
        ],
        out_specs=pl.BlockSpec((tq, MLA_WIDTH), lambda bi, s, qt, kt, ft, lt: (bi * nq + qt[s], 0)),
        scratch_shapes=[pltpu.VMEM((rows, LANES), F32), pltpu.VMEM((rows, LANES), F32),
                        pltpu.VMEM((rows, MLA_KV_RANK), F32)],
    )
    return pl.pallas_call(
        functools.partial(_mlap_kernel, tq=tq, tk=tk), grid_spec=grid_spec,
        out_shape=jax.ShapeDtypeStruct((n, MLA_WIDTH), F32),
        compiler_params=_cparams(("parallel", "arbitrary")), name="mlap",
    )(*tabs, qlat, qrope, ckv, kr, wuv)


def _smp_kernel(pt_ref, qsb_ref, kn_ref, vn_ref, ql_ref, qr_ref, cn_ref, krn_ref, wuv_ref, tri_ref, *rest,
                npg, tn, tc):
    ks = rest[0:npg]
    vs = rest[npg:2 * npg]
    cs = rest[2 * npg:3 * npg]
    rs = rest[3 * npg:4 * npg]
    osb_ref, omla_ref = rest[4 * npg:4 * npg + 2]
    (kcat, vcat, ccat, rcat, qsb_s, ql_s, qr_s, acc_sb, run_ref, m_ref, l_ref, acc_m) = rest[4 * npg + 2:]
    j = pl.program_id(1)
    rows = SB_HEADS * tn

    @pl.when(j == 0)
    def _():
        qsb_s[...] = qsb_ref[...].astype(F32).reshape(rows, LANES).astype(BF16)
        ql_s[...] = ql_ref[...].astype(F32).reshape(rows, MLA_KV_RANK).astype(BF16)
        qr_s[...] = qr_ref[...].astype(F32).reshape(rows, MLA_ROPE_DIM).astype(BF16)

        def padded(ref):
            x = ref[...]
            return jnp.concatenate([x, jnp.zeros((PAGE_SIZE - tn, x.shape[1]), F32)], axis=0).astype(BF16)

        knb, vnb, cnb, krnb = padded(kn_ref), padded(vn_ref), padded(cn_ref), padded(krn_ref)
        tpos = lax.broadcasted_iota(jnp.int32, (SB_HEADS, tn, PAGE_SIZE), 1).reshape(rows, PAGE_SIZE)
        spos = lax.broadcasted_iota(jnp.int32, (rows, PAGE_SIZE), 1)
        z = _dot_nt(qsb_s[...], knb)
        w, tot = _sb_tile(z, tri_ref[0:PAGE_SIZE, 0:PAGE_SIZE], jnp.zeros((rows, PAGE_SIZE), F32), spos < tpos)
        acc_sb[...] = _dot(w.astype(BF16), vnb)
        run_ref[...] = jnp.broadcast_to(tot, (rows, LANES))
        s = _dot_nt(ql_s[...], cnb) + _dot_nt(qr_s[...], krnb)
        s = jnp.where(spos <= tpos, s, NEG_BIG)
        m_ref[...] = jnp.full_like(m_ref, NEG_BIG)
        l_ref[...] = jnp.zeros_like(l_ref)
        acc_m[...] = jnp.zeros_like(acc_m)
        _mla_update(s, cnb, m_ref, l_ref, acc_m)

    for i in range(npg):
        sl = slice(i * PAGE_SIZE, (i + 1) * PAGE_SIZE)
        kcat[sl, :] = ks[i][0, 0].astype(BF16)
        vcat[sl, :] = vs[i][0, 0].astype(BF16)
        ccat[sl, :] = cs[i][0, 0].astype(BF16)
        rcat[sl, :] = rs[i][0, 0].astype(BF16)

    z = _dot_nt(qsb_s[...], kcat[...])
    tri = tri_ref[...]
    run = run_ref[...]
    acc = acc_sb[...]
    for c in range(npg * PAGE_SIZE // tc - 1, -1, -1):
        w, tot = _sb_tile(z[:, c * tc:(c + 1) * tc], tri, _rep_lanes(run, tc), None)
        acc = acc + _dot(w.astype(BF16), vcat[c * tc:(c + 1) * tc, :])
        run = run + tot
    acc_sb[...] = acc
    run_ref[...] = run

    cb = ccat[...]
    s = _dot_nt(ql_s[...], cb) + _dot_nt(qr_s[...], rcat[...])
    _mla_update(s, cb, m_ref, l_ref, acc_m)

    @pl.when(j == pl.num_programs(1) - 1)
    def _():
        a = acc_sb[...]
        half = SB_GROUP * tn
        _sb_pair_store(osb_ref, a[:half], a[half:], tn)
        _mla_finish(omla_ref, wuv_ref, l_ref, acc_m, tn)


def _smp_call(layer, page_table, qsb, kn, vn, qlat, qrope, cn, krn, wuv, cache_k, cache_v, cache_c, cache_r, tn):
    nseq, npages = page_table.shape
    npg = min(8, npages)
    tc = min(256, npg * PAGE_SIZE)
    nsteps = npages // npg
    rows = SB_HEADS * tn

    def seq3(s, j, pt):
        return (0, s, 0)

    def seq2(s, j, pt):
        return (s, 0)

    def page_map(i):
        return lambda s, j, pt: (layer, pt[s, npages - npg * (j + 1) + i], 0, 0)

    in_specs = [
        pl.BlockSpec((SB_HEADS, tn, LANES), seq3),
        pl.BlockSpec((tn, SB_KV_WIDTH), seq2),
        pl.BlockSpec((tn, SB_KV_WIDTH), seq2),
        pl.BlockSpec((MLA_HEADS, tn, MLA_KV_RANK), seq3),
        pl.BlockSpec((MLA_HEADS, tn, MLA_ROPE_DIM), seq3),
        pl.BlockSpec((tn, MLA_KV_RANK), seq2),
        pl.BlockSpec((tn, MLA_ROPE_DIM), seq2),
        pl.BlockSpec(wuv.shape, lambda s, j, pt: (0, 0, 0)),
        pl.BlockSpec((tc, tc), lambda s, j, pt: (0, 0)),
    ]
    operands = [qsb, kn, vn, qlat, qrope, cn, krn, wuv, _tri_matrix(tc)]
    for cache, width in ((cache_k, SB_KV_WIDTH), (cache_v, SB_KV_WIDTH), (cache_c, MLA_KV_RANK),
                         (cache_r, MLA_ROPE_DIM)):
        for i in range(npg):
            in_specs.append(pl.BlockSpec((1, 1, PAGE_SIZE, width), page_map(i)))
            operands.append(cache)
    tk = npg * PAGE_SIZE
    grid_spec = pltpu.PrefetchScalarGridSpec(
        num_scalar_prefetch=1, grid=(nseq, nsteps), in_specs=in_specs,
        out_specs=[pl.BlockSpec((tn, SB_WIDTH), seq2), pl.BlockSpec((tn, MLA_WIDTH), seq2)],
        scratch_shapes=[
            pltpu.VMEM((tk, SB_KV_WIDTH), BF16), pltpu.VMEM((tk, SB_KV_WIDTH), BF16),
            pltpu.VMEM((tk, MLA_KV_RANK), BF16), pltpu.VMEM((tk, MLA_ROPE_DIM), BF16),
            pltpu.VMEM((rows, LANES), BF16), pltpu.VMEM((rows, MLA_KV_RANK), BF16),
            pltpu.VMEM((rows, MLA_ROPE_DIM), BF16),
            pltpu.VMEM((rows, LANES), F32), pltpu.VMEM((rows, LANES), F32),
            pltpu.VMEM((rows, LANES), F32), pltpu.VMEM((rows, LANES), F32),
            pltpu.VMEM((rows, MLA_KV_RANK), F32),
        ],
    )
    n = nseq * tn
    return pl.pallas_call(
        functools.partial(_smp_kernel, npg=npg, tn=tn, tc=tc), grid_spec=grid_spec,
        out_shape=[jax.ShapeDtypeStruct((n, SB_WIDTH), F32), jax.ShapeDtypeStruct((n, MLA_WIDTH), F32)],
        compiler_params=_cparams(("parallel", "arbitrary")), name="smp",
    )(page_table, *operands)


def _out_kernel(x_ref, brc_ref, osb_ref, omla_ref, ng_ref, wz_ref, wg_ref, woc_ref, wos_ref, wom_ref, wo_ref,
                fg_ref, o_ref, *, final):
    x = x_ref[...]
    ub = _rms(x, ng_ref[...]).astype(BF16)
    z = _dot(ub, wz_ref[...])
    merged = None
    branches = ((brc_ref, woc_ref, CONV_CH), (osb_ref, wos_ref, SB_WIDTH), (omla_ref, wom_ref, MLA_WIDTH))
    off = 0
    for i, (b_ref, w_ref, width) in enumerate(branches):
        br = (b_ref[...] * _silu(z[:, off:off + width])).astype(BF16)
        off += width
        gate = _sigmoid(_dot(ub, wg_ref[:, i * D_MODEL:(i + 1) * D_MODEL]))
        term = gate * _dot(br, w_ref[...])
        merged = term if merged is None else merged + term
    out = x + _dot(merged.astype(BF16), wo_ref[...])
    if final:
        out = _rms(out, fg_ref[...])
    o_ref[...] = out


def _out_call(x, brc, osb, omla, lw, final_g, final, tm):
    n = x.shape[0]
    row = lambda i: (i, 0)
    c2 = lambda i: (0, 0)
    wnames = ('wz', 'wg', 'woc', 'wos', 'wom', 'wo')
    in_specs = [pl.BlockSpec((tm, D_MODEL), row), pl.BlockSpec((tm, CONV_CH), row),
                pl.BlockSpec((tm, SB_WIDTH), row), pl.BlockSpec((tm, MLA_WIDTH), row),
                pl.BlockSpec((1, D_MODEL), c2)]
    in_specs += [pl.BlockSpec(lw[k].shape, c2) for k in wnames]
    in_specs += [pl.BlockSpec((1, D_MODEL), c2)]
    return pl.pallas_call(
        functools.partial(_out_kernel, final=final), grid=(n // tm,), in_specs=in_specs,
        out_specs=pl.BlockSpec((tm, D_MODEL), row), out_shape=jax.ShapeDtypeStruct((n, D_MODEL), F32),
        compiler_params=_cparams(("parallel",)), name="out",
    )(x, brc, osb, omla, lw['norm_g'], *[lw[k] for k in wnames], final_g)


def _layer_weights(l, norm_g, w_in, conv_w, conv_b, conv_ln_g, conv_ln_b, mla_q_norm_g, mla_w_uq, mla_kv_norm_g,
                   mla_w_uk, mla_w_uv, w_out_conv, w_out_sb, w_out_mla, w_o):
    wi = w_in[l]
    col = lambda i: wi[:, IN_OFF[i]:IN_OFF[i + 1]]
    half = MLA_ROPE_DIM // 2
    wq = col(3).reshape(D_MODEL, SB_HEADS, SB_HEAD_DIM)
    zq = jnp.zeros_like(wq)
    grp = (jnp.arange(SB_HEADS) // SB_GROUP)[None, :, None]
    wq_pad = jnp.where(grp == 0, jnp.concatenate([wq, zq], -1), jnp.concatenate([zq, wq], -1))
    wsb = jnp.concatenate([wq_pad.reshape(D_MODEL, SB_HEADS * LANES), col(4), col(5)], axis=1)
    wkr = col(9)
    wkr_sw = jnp.concatenate([-wkr[:, half:], wkr[:, :half]], axis=1)
    padk = jnp.zeros((D_MODEL, LANES - MLA_ROPE_DIM), F32)
    wmla = jnp.concatenate([col(7), col(8), wkr, padk, wkr_sw, padk], axis=1)
    wuq = mla_w_uq[l].reshape(MLA_Q_RANK, MLA_HEADS, MLA_NOPE_DIM + MLA_ROPE_DIM)
    wuq_n = wuq[:, :, :MLA_NOPE_DIM].reshape(MLA_Q_RANK, MLA_HEADS * MLA_NOPE_DIM)
    wuq_r = wuq[:, :, MLA_NOPE_DIM:]
    wuq_r1 = wuq_r.reshape(MLA_Q_RANK, MLA_HEADS * MLA_ROPE_DIM)
    wuq_r2 = jnp.concatenate([-wuq_r[:, :, half:], wuq_r[:, :, :half]], axis=-1).reshape(MLA_Q_RANK, -1)
    wuq_cat = jnp.concatenate([wuq_n, wuq_r1, wuq_r2], axis=1)
    ukt = jnp.transpose(mla_w_uk[l], (1, 2, 0))
    zk = jnp.zeros_like(ukt)
    odd = (jnp.arange(MLA_HEADS) % 2)[:, None, None]
    wuk = jnp.where(odd == 0, jnp.concatenate([ukt, zk], 1), jnp.concatenate([zk, ukt], 1))
    wuv2 = mla_w_uv[l].reshape(MLA_KV_RANK, MLA_WIDTH)
    hcol = (jnp.arange(MLA_WIDTH) // MLA_V_DIM)[None, None, :]
    wuv = jnp.where(hcol == jnp.arange(MLA_HEADS)[:, None, None], wuv2[None], 0.0)
    order = np.array(SB_HEAD_ORDER)
    wz_sb = col(6).reshape(D_MODEL, SB_HEADS, SB_HEAD_DIM)[:, order].reshape(D_MODEL, SB_WIDTH)
    wos = w_out_sb[l].reshape(SB_HEADS, SB_HEAD_DIM, D_MODEL)[order].reshape(SB_WIDTH, D_MODEL)
    bf = lambda a: a.astype(BF16)
    return {
        'norm_g': norm_g[l][None], 'wa': bf(jnp.concatenate([col(0), col(1)], axis=1)), 'wsb': bf(wsb),
        'wmla': bf(wmla), 'qg': mla_q_norm_g[l][None], 'wuq': bf(wuq_cat), 'kvg': mla_kv_norm_g[l][None],
        'wuk': bf(wuk), 'wuv': bf(wuv),
        'conv_w': conv_w[l], 'conv_b': conv_b[l][None], 'conv_ln_g': conv_ln_g[l][None],
        'conv_ln_b': conv_ln_b[l][None],
        'wz': bf(jnp.concatenate([col(2), wz_sb, col(10)], axis=1)), 'wg': bf(col(11)),
        'woc': bf(w_out_conv[l]), 'wos': bf(wos), 'wom': bf(w_out_mla[l]), 'wo': bf(w_o[l]),
    }


def _rope_tables(pos):
    inv = ROPE_THETA ** (-jnp.arange(0, MLA_ROPE_DIM, 2, dtype=F32) / MLA_ROPE_DIM)
    ang = pos.astype(F32)[:, None] * inv[None, :]
    cos = jnp.cos(ang)
    sin = jnp.sin(ang)
    return (jnp.tile(jnp.concatenate([cos, cos], axis=1), (1, MLA_HEADS)),
            jnp.tile(jnp.concatenate([sin, sin], axis=1), (1, MLA_HEADS)))


def kernel(x_prompt, x_sample, cache_sb_k, cache_sb_v, cache_mla_latent, cache_mla_rope, state_conv, page_table,
           norm_g, w_in, conv_w, conv_b, conv_ln_g, conv_ln_b, mla_q_norm_g, mla_w_uq, mla_kv_norm_g, mla_w_uk,
           mla_w_uv, w_out_conv, w_out_sb, w_out_mla, w_o, final_norm_g):
    b_p, seq, _ = x_prompt.shape
    b_s, t_new, _ = x_sample.shape
    depth = norm_g.shape[0]
    n_pool = cache_sb_k.shape[1]
    past_len = page_table.shape[1] * PAGE_SIZE
    n_p = b_p * seq
    n_s = b_s * t_new
    tm_p = min(256, n_p)
    tm_s = min(256, n_s)

    cos_p, sin_p = _rope_tables(jnp.arange(seq, dtype=jnp.int32))
    pos_s = past_len + (jnp.arange(tm_s, dtype=jnp.int32) % t_new)
    cos_s, sin_s = _rope_tables(pos_s)
    cache_k = cache_sb_k.reshape(depth, n_pool, PAGE_SIZE, SB_KV_WIDTH)
    cache_v = cache_sb_v.reshape(depth, n_pool, PAGE_SIZE, SB_KV_WIDTH)
    fg = final_norm_g[None]
    zero_prefix = jnp.zeros((b_p, CONV_HALO, CONV_CH), F32)
    halo_pad = CONV_HALO - (CONV_WIDTH - 1)

    xp = x_prompt.reshape(n_p, D_MODEL)
    xs = x_sample.reshape(n_s, D_MODEL)
    st_p, st_s = [], []
    for l in range(depth):
        lw = _layer_weights(l, norm_g, w_in, conv_w, conv_b, conv_ln_g, conv_ln_b, mla_q_norm_g, mla_w_uq,
                            mla_kv_norm_g, mla_w_uk, mla_w_uv, w_out_conv, w_out_sb, w_out_mla, w_o)
        final = l == depth - 1
        h, qsb, k, v, qlat, qrope, ckv, kr = _pre_call(xp, cos_p, sin_p, lw, tm_p)
        h3 = h.reshape(b_p, seq, CONV_CH)
        brc = _conv_call(h3, zero_prefix, lw).reshape(n_p, CONV_CH)
        osb = _sbp_call(qsb, k, v, b_p, seq)
        omla = _mlap_call(qlat, qrope, ckv, kr, lw['wuv'], b_p, seq)
        xp = _out_call(xp, brc, osb, omla, lw, fg, final, tm_p)
        hist_p = jnp.concatenate([jnp.zeros((b_p, CONV_WIDTH - 1, CONV_CH), F32), h3], axis=1)
        st_p.append((k.reshape(b_p, seq, SB_KV_HEADS, SB_HEAD_DIM), v.reshape(b_p, seq, SB_KV_HEADS, SB_HEAD_DIM),
                     ckv.reshape(b_p, seq, MLA_KV_RANK), kr.reshape(b_p, seq, MLA_ROPE_DIM),
                     hist_p[:, -(CONV_WIDTH - 1):]))
        h, qsb, k, v, qlat, qrope, ckv, kr = _pre_call(xs, cos_s, sin_s, lw, tm_s)
        h3 = h.reshape(b_s, t_new, CONV_CH)
        prefix = jnp.pad(state_conv[l], ((0, 0), (halo_pad, 0), (0, 0)))
        brc = _conv_call(h3, prefix, lw).reshape(n_s, CONV_CH)
        osb, omla = _smp_call(l, page_table, qsb, k, v, qlat, qrope, ckv, kr, lw['wuv'], cache_k, cache_v,
                              cache_mla_latent, cache_mla_rope, t_new)
        xs = _out_call(xs, brc, osb, omla, lw, fg, final, tm_s)
        hist_s = jnp.concatenate([state_conv[l], h3], axis=1)
        st_s.append((k.reshape(b_s, t_new, SB_KV_HEADS, SB_HEAD_DIM), v.reshape(b_s, t_new, SB_KV_HEADS, SB_HEAD_DIM),
                     ckv.reshape(b_s, t_new, MLA_KV_RANK), kr.reshape(b_s, t_new, MLA_ROPE_DIM),
                     hist_s[:, -(CONV_WIDTH - 1):]))

    outs = [xp.reshape(b_p, seq, D_MODEL), xs.reshape(b_s, t_new, D_MODEL)]
    for st in (st_p, st_s):
        for i in range(5):
            outs.append(jnp.stack([s[i] for s in st]))
    return tuple(outs)
```

```python
import functools

import numpy as np
import jax
import jax.numpy as jnp
from jax import lax
from jax.experimental import pallas as pl
from jax.experimental.pallas import tpu as pltpu

F32 = jnp.float32
BF16 = jnp.bfloat16

D_MODEL = 1024
PAGE_SIZE = 128
CONV_CH = 512
CONV_WIDTH = 31
SB_HEADS = 8
SB_KV_HEADS = 2
SB_GROUP = SB_HEADS // SB_KV_HEADS
SB_HEAD_DIM = 64
SB_WIDTH = SB_HEADS * SB_HEAD_DIM
SB_KV_WIDTH = SB_KV_HEADS * SB_HEAD_DIM
MLA_HEADS = 8
MLA_Q_RANK = 256
MLA_KV_RANK = 256
MLA_NOPE_DIM = 64
MLA_ROPE_DIM = 32
MLA_V_DIM = 64
MLA_WIDTH = MLA_HEADS * MLA_V_DIM
MLA_SCALE = (MLA_NOPE_DIM + MLA_ROPE_DIM) ** -0.5
ROPE_THETA = 10000.0
N_BRANCHES = 3
NORM_EPS = 1e-6
IN_SIZES = (CONV_CH, CONV_CH, CONV_CH, SB_WIDTH, SB_KV_WIDTH, SB_KV_WIDTH, SB_WIDTH, MLA_Q_RANK, MLA_KV_RANK,
            MLA_ROPE_DIM, MLA_WIDTH, N_BRANCHES * D_MODEL)
IN_OFF = tuple(int(v) for v in np.concatenate([[0], np.cumsum(IN_SIZES)]))

LANES = 128
CONV_HALO = 32
LOG2E = 1.4426950408889634
MLA_EXP2_SCALE = MLA_SCALE * LOG2E
NEG_BIG = -1e30
VMEM_LIMIT = 48 * 1024 * 1024
SB_HEAD_ORDER = tuple(h for p in range(SB_GROUP) for h in (p, SB_GROUP + p))


def _dot(a, b):
    return jnp.dot(a, b, preferred_element_type=F32)


def _dot_nt(a, b):
    return lax.dot_general(a, b, (((1,), (1,)), ((), ())), preferred_element_type=F32)


def _sigmoid(x):
    return 1.0 / (1.0 + jnp.exp(-x))


def _silu(x):
    return x * _sigmoid(x)


def _rms(x, g):
    return x * lax.rsqrt(jnp.mean(x * x, axis=-1, keepdims=True) + NORM_EPS) * g


def _rep_lanes(x, width):
    reps = width // LANES
    return x if reps == 1 else jnp.concatenate([x] * reps, axis=1)


def _cparams(sem):
    return pltpu.CompilerParams(dimension_semantics=sem, vmem_limit_bytes=VMEM_LIMIT)


def _pre_kernel(x_ref, cos_ref, sin_ref, ng_ref, wa_ref, wsb_ref, wmla_ref, qg_ref, wuq_ref, kvg_ref, wuk_ref,
                h_ref, qsb_ref, k_ref, v_ref, qlat_ref, qrope_ref, ckv_ref, kr_ref):
    ub = _rms(x_ref[...], ng_ref[...]).astype(BF16)
    ag = _dot(ub, wa_ref[...])
    h_ref[...] = ag[:, :CONV_CH] * _sigmoid(ag[:, CONV_CH:])
    sb = _dot(ub, wsb_ref[...])
    for hd in range(SB_HEADS):
        qsb_ref[hd] = (sb[:, LANES * hd:LANES * (hd + 1)] * (SB_HEAD_DIM ** -0.5)).astype(BF16)
    k_ref[...] = sb[:, SB_HEADS * LANES:SB_HEADS * LANES + SB_KV_WIDTH]
    v_ref[...] = sb[:, SB_HEADS * LANES + SB_KV_WIDTH:]
    m = _dot(ub, wmla_ref[...])
    cos = cos_ref[...]
    sin = sin_ref[...]
    ckv_ref[...] = _rms(m[:, MLA_Q_RANK:MLA_Q_RANK + MLA_KV_RANK], kvg_ref[...])
    o1 = MLA_Q_RANK + MLA_KV_RANK
    kr_ref[...] = (m[:, o1:o1 + MLA_ROPE_DIM] * cos[:, :MLA_ROPE_DIM]
                   + m[:, o1 + LANES:o1 + LANES + MLA_ROPE_DIM] * sin[:, :MLA_ROPE_DIM])
    cqn = _rms(m[:, :MLA_Q_RANK], qg_ref[...]).astype(BF16)
    qq = _dot(cqn, wuq_ref[...])
    nw = MLA_HEADS * MLA_NOPE_DIM
    rw = MLA_HEADS * MLA_ROPE_DIM
    rot = qq[:, nw:nw + rw] * cos + qq[:, nw + rw:] * sin
    for hd in range(MLA_HEADS):
        qrope_ref[hd] = rot[:, MLA_ROPE_DIM * hd:MLA_ROPE_DIM * (hd + 1)].astype(BF16)
        pair = hd // 2
        qn = qq[:, LANES * pair:LANES * (pair + 1)].astype(BF16)
        qlat_ref[hd] = _dot(qn, wuk_ref[hd]).astype(BF16)


def _pre_call(x, cos, sin, lw, tm):
    n = x.shape[0]
    nblk = cos.shape[0] // tm
    row = lambda i: (i, 0)
    hrow = lambda i: (0, i, 0)
    c2 = lambda i: (0, 0)
    c3 = lambda i: (0, 0, 0)
    tab = lambda i: (i % nblk, 0)
    rw = MLA_HEADS * MLA_ROPE_DIM
    in_specs = [
        pl.BlockSpec((tm, D_MODEL), row),
        pl.BlockSpec((tm, rw), tab),
        pl.BlockSpec((tm, rw), tab),
        pl.BlockSpec((1, D_MODEL), c2),
        pl.BlockSpec(lw['wa'].shape, c2),
        pl.BlockSpec(lw['wsb'].shape, c2),
        pl.BlockSpec(lw['wmla'].shape, c2),
        pl.BlockSpec((1, MLA_Q_RANK), c2),
        pl.BlockSpec(lw['wuq'].shape, c2),
        pl.BlockSpec((1, MLA_KV_RANK), c2),
        pl.BlockSpec(lw['wuk'].shape, c3),
    ]
    out_shape = [
        jax.ShapeDtypeStruct((n, CONV_CH), F32),
        jax.ShapeDtypeStruct((SB_HEADS, n, LANES), BF16),
        jax.ShapeDtypeStruct((n, SB_KV_WIDTH), F32),
        jax.ShapeDtypeStruct((n, SB_KV_WIDTH), F32),
        jax.ShapeDtypeStruct((MLA_HEADS, n, MLA_KV_RANK), BF16),
        jax.ShapeDtypeStruct((MLA_HEADS, n, MLA_ROPE_DIM), BF16),
        jax.ShapeDtypeStruct((n, MLA_KV_RANK), F32),
        jax.ShapeDtypeStruct((n, MLA_ROPE_DIM), F32),
    ]
    out_specs = [
        pl.BlockSpec((tm, CONV_CH), row),
        pl.BlockSpec((SB_HEADS, tm, LANES), hrow),
        pl.BlockSpec((tm, SB_KV_WIDTH), row),
        pl.BlockSpec((tm, SB_KV_WIDTH), row),
        pl.BlockSpec((MLA_HEADS, tm, MLA_KV_RANK), hrow),
        pl.BlockSpec((MLA_HEADS, tm, MLA_ROPE_DIM), hrow),
        pl.BlockSpec((tm, MLA_KV_RANK), row),
        pl.BlockSpec((tm, MLA_ROPE_DIM), row),
    ]
    return pl.pallas_call(
        _pre_kernel, grid=(n // tm,), in_specs=in_specs, out_specs=out_specs, out_shape=out_shape,
        compiler_params=_cparams(("parallel",)), name="pre",
    )(x, cos, sin, lw['norm_g'], lw['wa'], lw['wsb'], lw['wmla'], lw['qg'], lw['wuq'], lw['kvg'], lw['wuk'])


def _conv_kernel(h_ref, pre_ref, w_ref, b_ref, g_ref, beta_ref, o_ref, win_ref, y_ref, *, tt, rc):
    t = pl.program_id(1)

    @pl.when(t == 0)
    def _():
        win_ref[0:CONV_HALO, :] = pre_ref[0]

    win_ref[CONV_HALO:CONV_HALO + tt, :] = h_ref[0]
    shift = CONV_HALO - (CONV_WIDTH - 1)
    for r in range(tt // rc):
        for c in range(CONV_CH // LANES):
            cs = slice(c * LANES, (c + 1) * LANES)
            acc = jnp.broadcast_to(b_ref[:, cs], (rc, LANES))
            for k in range(CONV_WIDTH):
                lo = r * rc + shift + k
                acc = acc + w_ref[k:k + 1, cs] * win_ref[lo:lo + rc, cs]
            y_ref[r * rc:(r + 1) * rc, cs] = acc
    y = y_ref[...]
    mu = jnp.mean(y, axis=-1, keepdims=True)
    yc = y - mu
    ln = yc * lax.rsqrt(jnp.mean(yc * yc, axis=-1, keepdims=True) + NORM_EPS) * g_ref[...] + beta_ref[...]
    o_ref[0] = _silu(ln)
    win_ref[0:CONV_HALO, :] = win_ref[tt:tt + CONV_HALO, :]


def _conv_call(h, prefix, lw):
    b, t, _ = h.shape
    tt = min(256, t)
    rc = min(32, tt)
    c2 = lambda i, j: (0, 0)
    kern = functools.partial(_conv_kernel, tt=tt, rc=rc)
    return pl.pallas_call(
        kern, grid=(b, t // tt),
        in_specs=[
            pl.BlockSpec((1, tt, CONV_CH), lambda i, j: (i, j, 0)),
            pl.BlockSpec((1, CONV_HALO, CONV_CH), lambda i, j: (i, 0, 0)),
            pl.BlockSpec((CONV_WIDTH, CONV_CH), c2),
            pl.BlockSpec((1, CONV_CH), c2),
            pl.BlockSpec((1, CONV_CH), c2),
            pl.BlockSpec((1, CONV_CH), c2),
        ],
        out_specs=pl.BlockSpec((1, tt, CONV_CH), lambda i, j: (i, j, 0)),
        out_shape=jax.ShapeDtypeStruct((b, t, CONV_CH), F32),
        scratch_shapes=[pltpu.VMEM((tt + CONV_HALO, CONV_CH), F32), pltpu.VMEM((tt, CONV_CH), F32)],
        compiler_params=_cparams(("parallel", "arbitrary")), name="conv",
    )(h, prefix, lw['conv_w'], lw['conv_b'], lw['conv_ln_g'], lw['conv_ln_b'])


def _sb_logits(z):
    z2 = z * LOG2E
    return z2, jnp.maximum(z2, 0.0) + jnp.log2(1.0 + jnp.exp2(-jnp.abs(z2)))


def _split_bf16(x):
    hi = x.astype(BF16)
    return hi, (x - hi.astype(F32)).astype(BF16)


def _sb_tile(z, tri, run, mask):
    z2, sp = _sb_logits(z)
    if mask is not None:
        sp = jnp.where(mask, sp, 0.0)
    hi, lo = _split_bf16(sp)
    cum = _dot(hi, tri) + _dot(lo, tri)
    w = jnp.exp2(z2 - sp - cum - run)
    if mask is not None:
        w = jnp.where(mask, w, 0.0)
    return w, jnp.sum(sp, axis=1, keepdims=True)


def _tri_matrix(tk, ones_cols=0):
    j = np.arange(tk)[:, None]
    s = np.arange(tk + ones_cols)[None, :]
    return jnp.asarray(((j > s) | (s >= tk)).astype(np.float32), dtype=BF16)


def _sb_pair_store(o_ref, acc0, acc1, rows):
    lane = lax.broadcasted_iota(jnp.int32, (rows, LANES), 1)
    for p in range(SB_GROUP):
        o_ref[:, LANES * p:LANES * (p + 1)] = jnp.where(lane < SB_HEAD_DIM, acc0[p * rows:(p + 1) * rows],
                                                         acc1[p * rows:(p + 1) * rows])


def _sbp_kernel(qi_ref, kb_ref, first_ref, last_ref, q_ref, k_ref, v_ref, tri_ref, o_ref, acc_ref, run_ref, *, tq):
    step = pl.program_id(1)
    rows = SB_GROUP * tq

    @pl.when(first_ref[step] == 1)
    def _():
        acc_ref[...] = jnp.zeros_like(acc_ref)
        run_ref[...] = jnp.zeros_like(run_ref)

    def body(masked):
        kb = k_ref[...].astype(BF16)
        vb = v_ref[...].astype(BF16)
        tri = tri_ref[...]
        mask = None
        if masked:
            tpos = lax.broadcasted_iota(jnp.int32, (SB_GROUP, tq, tq), 1).reshape(rows, tq)
            spos = lax.broadcasted_iota(jnp.int32, (rows, tq), 1)
            mask = spos < tpos
        for g in range(SB_KV_HEADS):
            q = q_ref[SB_GROUP * g:SB_GROUP * (g + 1)].reshape(rows, LANES)
            z = _dot_nt(q, kb)
            run = run_ref[g]
            w, tot = _sb_tile(z, tri, _rep_lanes(run, tq), mask)
            acc_ref[g] += _dot(w.astype(BF16), vb)
            run_ref[g] = run + tot

    diag = qi_ref[step] == kb_ref[step]

    @pl.when(diag)
    def _():
        body(True)

    @pl.when(jnp.logical_not(diag))
    def _():
        body(False)

    @pl.when(last_ref[step] == 1)
    def _():
        _sb_pair_store(o_ref, acc_ref[0], acc_ref[1], tq)


def _sbp_call(qsb, k, v, b, t):
    tq = min(256, t)
    nq = t // tq
    qi_l, kb_l, first_l, last_l = [], [], [], []
    for qi in range(nq):
        for kb in range(qi, -1, -1):
            qi_l.append(qi)
            kb_l.append(kb)
            first_l.append(int(kb == qi))
            last_l.append(int(kb == 0))
    tabs = [jnp.asarray(np.array(a, np.int32)) for a in (qi_l, kb_l, first_l, last_l)]
    n = b * t
    grid_spec = pltpu.PrefetchScalarGridSpec(
        num_scalar_prefetch=4, grid=(b, len(qi_l)),
        in_specs=[
            pl.BlockSpec((SB_HEADS, tq, LANES), lambda bi, s, qt, kt, ft, lt: (0, bi * nq + qt[s], 0)),
            pl.BlockSpec((tq, SB_KV_WIDTH), lambda bi, s, qt, kt, ft, lt: (bi * nq + kt[s], 0)),
            pl.BlockSpec((tq, SB_KV_WIDTH), lambda bi, s, qt, kt, ft, lt: (bi * nq + kt[s], 0)),
            pl.BlockSpec((tq, tq), lambda bi, s, qt, kt, ft, lt: (0, 0)),
        ],
        out_specs=pl.BlockSpec((tq, SB_WIDTH), lambda bi, s, qt, kt, ft, lt: (bi * nq + qt[s], 0)),
        scratch_shapes=[pltpu.VMEM((SB_KV_HEADS, SB_GROUP * tq, LANES), F32),
                        pltpu.VMEM((SB_KV_HEADS, SB_GROUP * tq, LANES), F32)],
    )
    return pl.pallas_call(
        functools.partial(_sbp_kernel, tq=tq), grid_spec=grid_spec,
        out_shape=jax.ShapeDtypeStruct((n, SB_WIDTH), F32),
        compiler_params=_cparams(("parallel", "arbitrary")), name="sbp",
    )(*tabs, qsb, k, v, _tri_matrix(tq))


def _mla_update(s, cb, m_ref, l_ref, acc_ref):
    tk = s.shape[1]
    m_prev = m_ref[...]
    m_new = jnp.maximum(m_prev, jnp.max(s, axis=1, keepdims=True))
    p = jnp.exp2((s - _rep_lanes(m_new, tk)) * MLA_EXP2_SCALE)
    alpha = jnp.exp2((m_prev - m_new) * MLA_EXP2_SCALE)
    l_ref[...] = alpha * l_ref[...] + jnp.sum(p, axis=1, keepdims=True)
    acc_ref[...] = _rep_lanes(alpha, acc_ref.shape[1]) * acc_ref[...] + _dot(p.astype(BF16), cb)
    m_ref[...] = m_new


def _mla_finish(o_ref, wuv_ref, l_ref, acc_ref, rows):
    inv = _rep_lanes(1.0 / l_ref[...], acc_ref.shape[1])
    o = None
    for hd in range(MLA_HEADS):
        rs = slice(hd * rows, (hd + 1) * rows)
        oh = _dot((acc_ref[rs] * inv[rs]).astype(BF16), wuv_ref[hd])
        o = oh if o is None else o + oh
    o_ref[...] = o


def _mlap_kernel(qi_ref, kb_ref, first_ref, last_ref, ql_ref, qr_ref, c_ref, kr_ref, wuv_ref, o_ref,
                 m_ref, l_ref, acc_ref, *, tq, tk):
    step = pl.program_id(1)
    rows = MLA_HEADS * tq
    qi = qi_ref[step]
    kb = kb_ref[step]

    @pl.when(first_ref[step] == 1)
    def _():
        m_ref[...] = jnp.full_like(m_ref, NEG_BIG)
        l_ref[...] = jnp.zeros_like(l_ref)
        acc_ref[...] = jnp.zeros_like(acc_ref)

    def body(masked):
        cb = c_ref[...].astype(BF16)
        krb = kr_ref[...].astype(BF16)
        ql = ql_ref[...].reshape(rows, MLA_KV_RANK)
        qr = qr_ref[...].reshape(rows, MLA_ROPE_DIM)
        s = _dot_nt(ql, cb) + _dot_nt(qr, krb)
        if masked:
            tpos = qi * tq + lax.broadcasted_iota(jnp.int32, (MLA_HEADS, tq, tk), 1).reshape(rows, tk)
            spos = kb * tk + lax.broadcasted_iota(jnp.int32, (rows, tk), 1)
            s = jnp.where(spos <= tpos, s, NEG_BIG)
        _mla_update(s, cb, m_ref, l_ref, acc_ref)

    straddle = (kb + 1) * tk - 1 > qi * tq

    @pl.when(straddle)
    def _():
        body(True)

    @pl.when(jnp.logical_not(straddle))
    def _():
        body(False)

    @pl.when(last_ref[step] == 1)
    def _():
        _mla_finish(o_ref, wuv_ref, l_ref, acc_ref, tq)


def _mlap_call(qlat, qrope, ckv, kr, wuv, b, t):
    tq = min(128, t)
    tk = min(256, t)
    nq = t // tq
    nk = t // tk
    qi_l, kb_l, first_l, last_l = [], [], [], []
    for qi in range(nq):
        kmax = ((qi + 1) * tq - 1) // tk
        for kb in range(kmax + 1):
            qi_l.append(qi)
            kb_l.append(kb)
            first_l.append(int(kb == 0))
            last_l.append(int(kb == kmax))
    tabs = [jnp.asarray(np.array(a, np.int32)) for a in (qi_l, kb_l, first_l, last_l)]
    n = b * t
    rows = MLA_HEADS * tq
    grid_spec = pltpu.PrefetchScalarGridSpec(
        num_scalar_prefetch=4, grid=(b, len(qi_l)),
        in_specs=[
            pl.BlockSpec((MLA_HEADS, tq, MLA_KV_RANK), lambda bi, s, qt, kt, ft, lt: (0, bi * nq + qt[s], 0)),
            pl.BlockSpec((MLA_HEADS, tq, MLA_ROPE_DIM), lambda bi, s, qt, kt, ft, lt: (0, bi * nq + qt[s], 0)),
            pl.BlockSpec((tk, MLA_KV_RANK), lambda bi, s, qt, kt, ft, lt: (bi * nk + kt[s], 0)),
            pl.BlockSpec((tk, MLA_ROPE_DIM), lambda bi, s, qt, kt, ft, lt: (bi * nk + kt[s], 0)),
            pl.BlockSpec(wuv.shape, lambda bi, s, qt, kt, ft, lt: (0, 0, 0)),
        ],
        out_specs=pl.BlockSpec((tq, MLA_WIDTH), lambda bi, s, qt, kt, ft, lt: (bi * nq + qt[s], 0)),
        scratch_shapes=[pltpu.VMEM((rows, LANES), F32), pltpu.VMEM((rows, LANES), F32),
                        pltpu.VMEM((rows, MLA_KV_RANK), F32)],
    )
    return pl.pallas_call(
        functools.partial(_mlap_kernel, tq=tq, tk=tk), grid_spec=grid_spec,
        out_shape=jax.ShapeDtypeStruct((n, MLA_WIDTH), F32),
        compiler_params=_cparams(("parallel", "arbitrary")), name="mlap",
    )(*tabs, qlat, qrope, ckv, kr, wuv)


def _smp_kernel(pt_ref, qsb_ref, kn_ref, vn_ref, ql_ref, qr_ref, cn_ref, krn_ref, wuv_ref, tri_ref, *rest,
                npg, tn, tc):
    kts = rest[0:npg]
    vts = rest[npg:2 * npg]
    cs = rest[2 * npg:3 * npg]
    rts = rest[3 * npg:4 * npg]
    osb_ref, omla_ref = rest[4 * npg:4 * npg + 2]
    qsb_s, ql_s, qr_s, acc_sb, run_ref, m_ref, l_ref, acc_m = rest[4 * npg + 2:]
    j = pl.program_id(1)
    rows = SB_HEADS * tn
    ppc = tc // PAGE_SIZE
    nch = npg // ppc

    @pl.when(j == 0)
    def _():
        qsb_s[...] = qsb_ref[...].astype(F32).reshape(rows, LANES).astype(BF16)
        ql_s[...] = ql_ref[...].astype(F32).reshape(rows, MLA_KV_RANK).astype(BF16)
        qr_s[...] = qr_ref[...].astype(F32).reshape(rows, MLA_ROPE_DIM).astype(BF16)

        def padded(ref):
            x = ref[...]
            return jnp.concatenate([x, jnp.zeros((PAGE_SIZE - tn, x.shape[1]), F32)], axis=0).astype(BF16)

        knb, vnb, cnb, krnb = padded(kn_ref), padded(vn_ref), padded(cn_ref), padded(krn_ref)
        tpos = lax.broadcasted_iota(jnp.int32, (SB_HEADS, tn, PAGE_SIZE), 1).reshape(rows, PAGE_SIZE)
        spos = lax.broadcasted_iota(jnp.int32, (rows, PAGE_SIZE), 1)
        z = _dot_nt(qsb_s[...], knb)
        w, tot = _sb_tile(z, tri_ref[0:PAGE_SIZE, 0:PAGE_SIZE], jnp.zeros((rows, PAGE_SIZE), F32), spos < tpos)
        acc_sb[...] = _dot(w.astype(BF16), vnb)
        run_ref[...] = jnp.broadcast_to(tot, (rows, LANES))
        s = _dot_nt(ql_s[...], cnb) + _dot_nt(qr_s[...], krnb)
        s = jnp.where(spos <= tpos, s, NEG_BIG)
        m_ref[...] = jnp.full_like(m_ref, NEG_BIG)
        l_ref[...] = jnp.zeros_like(l_ref)
        acc_m[...] = jnp.zeros_like(acc_m)
        _mla_update(s, cnb, m_ref, l_ref, acc_m)

    def page_slice(x, i):
        return x[:, i * PAGE_SIZE:(i + 1) * PAGE_SIZE]

    qsb = qsb_s[...]
    z = jnp.concatenate([_dot(qsb, kts[i][0, 0].astype(BF16)) for i in range(npg)], axis=1)
    z2, sp = _sb_logits(z)
    hi, lo = _split_bf16(sp)
    chunks = lambda x: [x[:, c * tc:(c + 1) * tc] for c in range(nch)]
    res = _dot(jnp.concatenate(chunks(hi) + chunks(lo), axis=0), tri_ref[...])
    cum = res[:nch * rows] + res[nch * rows:]
    arg = z2 - sp
    run = run_ref[...]
    acc = acc_sb[...]
    for c in range(nch - 1, -1, -1):
        cc = cum[c * rows:(c + 1) * rows]
        w = jnp.exp2(arg[:, c * tc:(c + 1) * tc] - cc[:, :tc] - _rep_lanes(run, tc)).astype(BF16)
        for i in range(ppc):
            acc = acc + _dot_nt(page_slice(w, i), vts[c * ppc + i][0, 0].astype(BF16))
        run = run + cc[:, tc:]
    acc_sb[...] = acc
    run_ref[...] = run

    ql = ql_s[...]
    qr = qr_s[...]
    s = jnp.concatenate([_dot_nt(ql, cs[i][0, 0].astype(BF16)) + _dot(qr, rts[i][0, 0].astype(BF16))
                         for i in range(npg)], axis=1)
    m_prev = m_ref[...]
    m_new = jnp.maximum(m_prev, jnp.max(s, axis=1, keepdims=True))
    p = jnp.exp2((s - _rep_lanes(m_new, npg * PAGE_SIZE)) * MLA_EXP2_SCALE)
    alpha = jnp.exp2((m_prev - m_new) * MLA_EXP2_SCALE)
    l_ref[...] = alpha * l_ref[...] + jnp.sum(p, axis=1, keepdims=True)
    pb = p.astype(BF16)
    pv = None
    for i in range(npg):
        t = _dot(page_slice(pb, i), cs[i][0, 0].astype(BF16))
        pv = t if pv is None else pv + t
    acc_m[...] = _rep_lanes(alpha, MLA_KV_RANK) * acc_m[...] + pv
    m_ref[...] = m_new

    @pl.when(j == pl.num_programs(1) - 1)
    def _():
        a = acc_sb[...]
        half = SB_GROUP * tn
        _sb_pair_store(osb_ref, a[:half], a[half:], tn)
        _mla_finish(omla_ref, wuv_ref, l_ref, acc_m, tn)


def _smp_call(layer, page_table, qsb, kn, vn, qlat, qrope, cn, krn, wuv, cache_kt, cache_vt, cache_c, cache_rt, tn):
    nseq, npages = page_table.shape
    npg = min(16, npages)
    tc = min(256, npg * PAGE_SIZE)
    nsteps = npages // npg
    rows = SB_HEADS * tn

    def seq3(s, j, pt):
        return (0, s, 0)

    def seq2(s, j, pt):
        return (s, 0)

    def page_map(i):
        return lambda s, j, pt: (layer, pt[s, npages - npg * (j + 1) + i], 0, 0)

    triones = _tri_matrix(tc, LANES)
    in_specs = [
        pl.BlockSpec((SB_HEADS, tn, LANES), seq3),
        pl.BlockSpec((tn, SB_KV_WIDTH), seq2),
        pl.BlockSpec((tn, SB_KV_WIDTH), seq2),
        pl.BlockSpec((MLA_HEADS, tn, MLA_KV_RANK), seq3),
        pl.BlockSpec((MLA_HEADS, tn, MLA_ROPE_DIM), seq3),
        pl.BlockSpec((tn, MLA_KV_RANK), seq2),
        pl.BlockSpec((tn, MLA_ROPE_DIM), seq2),
        pl.BlockSpec(wuv.shape, lambda s, j, pt: (0, 0, 0)),
        pl.BlockSpec(triones.shape, lambda s, j, pt: (0, 0)),
    ]
    operands = [qsb, kn, vn, qlat, qrope, cn, krn, wuv, triones]
    for cache in (cache_kt, cache_vt, cache_c, cache_rt):
        for i in range(npg):
            in_specs.append(pl.BlockSpec((1, 1) + cache.shape[2:], page_map(i)))
            operands.append(cache)
    grid_spec = pltpu.PrefetchScalarGridSpec(
        num_scalar_prefetch=1, grid=(nseq, nsteps), in_specs=in_specs,
        out_specs=[pl.BlockSpec((tn, SB_WIDTH), seq2), pl.BlockSpec((tn, MLA_WIDTH), seq2)],
        scratch_shapes=[
            pltpu.VMEM((rows, LANES), BF16), pltpu.VMEM((rows, MLA_KV_RANK), BF16),
            pltpu.VMEM((rows, MLA_ROPE_DIM), BF16),
            pltpu.VMEM((rows, LANES), F32), pltpu.VMEM((rows, LANES), F32),
            pltpu.VMEM((rows, LANES), F32), pltpu.VMEM((rows, LANES), F32),
            pltpu.VMEM((rows, MLA_KV_RANK), F32),
        ],
    )
    n = nseq * tn
    return pl.pallas_call(
        functools.partial(_smp_kernel, npg=npg, tn=tn, tc=tc), grid_spec=grid_spec,
        out_shape=[jax.ShapeDtypeStruct((n, SB_WIDTH), F32), jax.ShapeDtypeStruct((n, MLA_WIDTH), F32)],
        compiler_params=_cparams(("parallel", "arbitrary")), name="smp",
    )(page_table, *operands)


def _out_kernel(x_ref, brc_ref, osb_ref, omla_ref, ng_ref, wz_ref, wg_ref, woc_ref, wos_ref, wom_ref, wo_ref,
                fg_ref, o_ref, *, final):
    x = x_ref[...]
    ub = _rms(x, ng_ref[...]).astype(BF16)
    z = _dot(ub, wz_ref[...])
    merged = None
    branches = ((brc_ref, woc_ref, CONV_CH), (osb_ref, wos_ref, SB_WIDTH), (omla_ref, wom_ref, MLA_WIDTH))
    off = 0
    for i, (b_ref, w_ref, width) in enumerate(branches):
        br = (b_ref[...] * _silu(z[:, off:off + width])).astype(BF16)
        off += width
        gate = _sigmoid(_dot(ub, wg_ref[:, i * D_MODEL:(i + 1) * D_MODEL]))
        term = gate * _dot(br, w_ref[...])
        merged = term if merged is None else merged + term
    out = x + _dot(merged.astype(BF16), wo_ref[...])
    if final:
        out = _rms(out, fg_ref[...])
    o_ref[...] = out


def _out_call(x, brc, osb, omla, lw, final_g, final, tm):
    n = x.shape[0]
    row = lambda i: (i, 0)
    c2 = lambda i: (0, 0)
    wnames = ('wz', 'wg', 'woc', 'wos', 'wom', 'wo')
    in_specs = [pl.BlockSpec((tm, D_MODEL), row), pl.BlockSpec((tm, CONV_CH), row),
                pl.BlockSpec((tm, SB_WIDTH), row), pl.BlockSpec((tm, MLA_WIDTH), row),
                pl.BlockSpec((1, D_MODEL), c2)]
    in_specs += [pl.BlockSpec(lw[k].shape, c2) for k in wnames]
    in_specs += [pl.BlockSpec((1, D_MODEL), c2)]
    return pl.pallas_call(
        functools.partial(_out_kernel, final=final), grid=(n // tm,), in_specs=in_specs,
        out_specs=pl.BlockSpec((tm, D_MODEL), row), out_shape=jax.ShapeDtypeStruct((n, D_MODEL), F32),
        compiler_params=_cparams(("parallel",)), name="out",
    )(x, brc, osb, omla, lw['norm_g'], *[lw[k] for k in wnames], final_g)


def _layer_weights(l, norm_g, w_in, conv_w, conv_b, conv_ln_g, conv_ln_b, mla_q_norm_g, mla_w_uq, mla_kv_norm_g,
                   mla_w_uk, mla_w_uv, w_out_conv, w_out_sb, w_out_mla, w_o):
    wi = w_in[l]
    col = lambda i: wi[:, IN_OFF[i]:IN_OFF[i + 1]]
    half = MLA_ROPE_DIM // 2
    wq = col(3).reshape(D_MODEL, SB_HEADS, SB_HEAD_DIM)
    zq = jnp.zeros_like(wq)
    grp = (jnp.arange(SB_HEADS) // SB_GROUP)[None, :, None]
    wq_pad = jnp.where(grp == 0, jnp.concatenate([wq, zq], -1), jnp.concatenate([zq, wq], -1))
    wsb = jnp.concatenate([wq_pad.reshape(D_MODEL, SB_HEADS * LANES), col(4), col(5)], axis=1)
    wkr = col(9)
    wkr_sw = jnp.concatenate([-wkr[:, half:], wkr[:, :half]], axis=1)
    padk = jnp.zeros((D_MODEL, LANES - MLA_ROPE_DIM), F32)
    wmla = jnp.concatenate([col(7), col(8), wkr, padk, wkr_sw, padk], axis=1)
    wuq = mla_w_uq[l].reshape(MLA_Q_RANK, MLA_HEADS, MLA_NOPE_DIM + MLA_ROPE_DIM)
    wuq_n = wuq[:, :, :MLA_NOPE_DIM].reshape(MLA_Q_RANK, MLA_HEADS * MLA_NOPE_DIM)
    wuq_r = wuq[:, :, MLA_NOPE_DIM:]
    wuq_r1 = wuq_r.reshape(MLA_Q_RANK, MLA_HEADS * MLA_ROPE_DIM)
    wuq_r2 = jnp.concatenate([-wuq_r[:, :, half:], wuq_r[:, :, :half]], axis=-1).reshape(MLA_Q_RANK, -1)
    wuq_cat = jnp.concatenate([wuq_n, wuq_r1, wuq_r2], axis=1)
    ukt = jnp.transpose(mla_w_uk[l], (1, 2, 0))
    zk = jnp.zeros_like(ukt)
    odd = (jnp.arange(MLA_HEADS) % 2)[:, None, None]
    wuk = jnp.where(odd == 0, jnp.concatenate([ukt, zk], 1), jnp.concatenate([zk, ukt], 1))
    wuv2 = mla_w_uv[l].reshape(MLA_KV_RANK, MLA_WIDTH)
    hcol = (jnp.arange(MLA_WIDTH) // MLA_V_DIM)[None, None, :]
    wuv = jnp.where(hcol == jnp.arange(MLA_HEADS)[:, None, None], wuv2[None], 0.0)
    order = np.array(SB_HEAD_ORDER)
    wz_sb = col(6).reshape(D_MODEL, SB_HEADS, SB_HEAD_DIM)[:, order].reshape(D_MODEL, SB_WIDTH)
    wos = w_out_sb[l].reshape(SB_HEADS, SB_HEAD_DIM, D_MODEL)[order].reshape(SB_WIDTH, D_MODEL)
    bf = lambda a: a.astype(BF16)
    return {
        'norm_g': norm_g[l][None], 'wa': bf(jnp.concatenate([col(0), col(1)], axis=1)), 'wsb': bf(wsb),
        'wmla': bf(wmla), 'qg': mla_q_norm_g[l][None], 'wuq': bf(wuq_cat), 'kvg': mla_kv_norm_g[l][None],
        'wuk': bf(wuk), 'wuv': bf(wuv),
        'conv_w': conv_w[l], 'conv_b': conv_b[l][None], 'conv_ln_g': conv_ln_g[l][None],
        'conv_ln_b': conv_ln_b[l][None],
        'wz': bf(jnp.concatenate([col(2), wz_sb, col(10)], axis=1)), 'wg': bf(col(11)),
        'woc': bf(w_out_conv[l]), 'wos': bf(wos), 'wom': bf(w_out_mla[l]), 'wo': bf(w_o[l]),
    }


def _rope_tables(pos):
    inv = ROPE_THETA ** (-jnp.arange(0, MLA_ROPE_DIM, 2, dtype=F32) / MLA_ROPE_DIM)
    ang = pos.astype(F32)[:, None] * inv[None, :]
    cos = jnp.cos(ang)
    sin = jnp.sin(ang)
    return (jnp.tile(jnp.concatenate([cos, cos], axis=1), (1, MLA_HEADS)),
            jnp.tile(jnp.concatenate([sin, sin], axis=1), (1, MLA_HEADS)))


def kernel(x_prompt, x_sample, cache_sb_k, cache_sb_v, cache_mla_latent, cache_mla_rope, state_conv, page_table,
           norm_g, w_in, conv_w, conv_b, conv_ln_g, conv_ln_b, mla_q_norm_g, mla_w_uq, mla_kv_norm_g, mla_w_uk,
           mla_w_uv, w_out_conv, w_out_sb, w_out_mla, w_o, final_norm_g):
    b_p, seq, _ = x_prompt.shape
    b_s, t_new, _ = x_sample.shape
    depth = norm_g.shape[0]
    n_pool = cache_sb_k.shape[1]
    past_len = page_table.shape[1] * PAGE_SIZE
    n_p = b_p * seq
    n_s = b_s * t_new
    tm_p = min(256, n_p)
    tm_s = min(256, n_s)

    cos_p, sin_p = _rope_tables(jnp.arange(seq, dtype=jnp.int32))
    pos_s = past_len + (jnp.arange(tm_s, dtype=jnp.int32) % t_new)
    cos_s, sin_s = _rope_tables(pos_s)
    cache_kt = jnp.transpose(cache_sb_k, (0, 1, 3, 4, 2)).reshape(depth, n_pool, SB_KV_WIDTH, PAGE_SIZE)
    cache_vt = jnp.transpose(cache_sb_v, (0, 1, 3, 4, 2)).reshape(depth, n_pool, SB_KV_WIDTH, PAGE_SIZE)
    cache_rt = jnp.transpose(cache_mla_rope, (0, 1, 3, 2))
    fg = final_norm_g[None]
    zero_prefix = jnp.zeros((b_p, CONV_HALO, CONV_CH), F32)
    halo_pad = CONV_HALO - (CONV_WIDTH - 1)

    xp = x_prompt.reshape(n_p, D_MODEL)
    xs = x_sample.reshape(n_s, D_MODEL)
    st_p, st_s = [], []
    for l in range(depth):
        lw = _layer_weights(l, norm_g, w_in, conv_w, conv_b, conv_ln_g, conv_ln_b, mla_q_norm_g, mla_w_uq,
                            mla_kv_norm_g, mla_w_uk, mla_w_uv, w_out_conv, w_out_sb, w_out_mla, w_o)
        final = l == depth - 1
        h, qsb, k, v, qlat, qrope, ckv, kr = _pre_call(xp, cos_p, sin_p, lw, tm_p)
        h3 = h.reshape(b_p, seq, CONV_CH)
        brc = _conv_call(h3, zero_prefix, lw).reshape(n_p, CONV_CH)
        osb = _sbp_call(qsb, k, v, b_p, seq)
        omla = _mlap_call(qlat, qrope, ckv, kr, lw['wuv'], b_p, seq)
        xp = _out_call(xp, brc, osb, omla, lw, fg, final, tm_p)
        hist_p = jnp.concatenate([jnp.zeros((b_p, CONV_WIDTH - 1, CONV_CH), F32), h3], axis=1)
        st_p.append((k.reshape(b_p, seq, SB_KV_HEADS, SB_HEAD_DIM), v.reshape(b_p, seq, SB_KV_HEADS, SB_HEAD_DIM),
                     ckv.reshape(b_p, seq, MLA_KV_RANK), kr.reshape(b_p, seq, MLA_ROPE_DIM),
                     hist_p[:, -(CONV_WIDTH - 1):]))
        h, qsb, k, v, qlat, qrope, ckv, kr = _pre_call(xs, cos_s, sin_s, lw, tm_s)
        h3 = h.reshape(b_s, t_new, CONV_CH)
        prefix = jnp.pad(state_conv[l], ((0, 0), (halo_pad, 0), (0, 0)))
        brc = _conv_call(h3, prefix, lw).reshape(n_s, CONV_CH)
        osb, omla = _smp_call(l, page_table, qsb, k, v, qlat, qrope, ckv, kr, lw['wuv'], cache_kt, cache_vt,
                              cache_mla_latent, cache_rt, t_new)
        xs = _out_call(xs, brc, osb, omla, lw, fg, final, tm_s)
        hist_s = jnp.concatenate([state_conv[l], h3], axis=1)
        st_s.append((k.reshape(b_s, t_new, SB_KV_HEADS, SB_HEAD_DIM), v.reshape(b_s, t_new, SB_KV_HEADS, SB_HEAD_DIM),
                     ckv.reshape(b_s, t_new, MLA_KV_RANK), kr.reshape(b_s, t_new, MLA_ROPE_DIM),
                     hist_s[:, -(CONV_WIDTH - 1):]))

    outs = [xp.reshape(b_p, seq, D_MODEL), xs.reshape(b_s, t_new, D_MODEL)]
    for st in (st_p, st_s):
        for i in range(5):
            outs.append(jnp.stack([s[i] for s in st]))
    return tuple(outs)
```

```python
import functools

import numpy as np
import jax
import jax.numpy as jnp
from jax import lax
from jax.experimental import pallas as pl
from jax.experimental.pallas import tpu as pltpu

F32 = jnp.float32
BF16 = jnp.bfloat16

D_MODEL = 1024
PAGE_SIZE = 128
CONV_CH = 512
CONV_WIDTH = 31
SB_HEADS = 8
SB_KV_HEADS = 2
SB_GROUP = SB_HEADS // SB_KV_HEADS
SB_HEAD_DIM = 64
SB_WIDTH = SB_HEADS * SB_HEAD_DIM
SB_KV_WIDTH = SB_KV_HEADS * SB_HEAD_DIM
MLA_HEADS = 8
MLA_Q_RANK = 256
MLA_KV_RANK = 256
MLA_NOPE_DIM = 64
MLA_ROPE_DIM = 32
MLA_V_DIM = 64
MLA_WIDTH = MLA_HEADS * MLA_V_DIM
MLA_SCALE = (MLA_NOPE_DIM + MLA_ROPE_DIM) ** -0.5
ROPE_THETA = 10000.0
N_BRANCHES = 3
NORM_EPS = 1e-6
IN_SIZES = (CONV_CH, CONV_CH, CONV_CH, SB_WIDTH, SB_KV_WIDTH, SB_KV_WIDTH, SB_WIDTH, MLA_Q_RANK, MLA_KV_RANK,
            MLA_ROPE_DIM, MLA_WIDTH, N_BRANCHES * D_MODEL)
IN_OFF = tuple(int(v) for v in np.concatenate([[0], np.cumsum(IN_SIZES)]))

LANES = 128
CONV_HALO = 32
LOG2E = 1.4426950408889634
MLA_EXP2_SCALE = MLA_SCALE * LOG2E
NEG_BIG = -1e30
SB_DEAD_LOG2 = 150.0
VMEM_LIMIT = 48 * 1024 * 1024
SB_HEAD_ORDER = tuple(h for p in range(SB_GROUP) for h in (p, SB_GROUP + p))


def _dot(a, b):
    return jnp.dot(a, b, preferred_element_type=F32)


def _dot_nt(a, b):
    return lax.dot_general(a, b, (((1,), (1,)), ((), ())), preferred_element_type=F32)


def _sigmoid(x):
    return 1.0 / (1.0 + jnp.exp(-x))


def _silu(x):
    return x * _sigmoid(x)


def _rms(x, g):
    return x * lax.rsqrt(jnp.mean(x * x, axis=-1, keepdims=True) + NORM_EPS) * g


def _rep_lanes(x, width):
    reps = width // LANES
    return x if reps == 1 else jnp.concatenate([x] * reps, axis=1)


def _cparams(sem):
    return pltpu.CompilerParams(dimension_semantics=sem, vmem_limit_bytes=VMEM_LIMIT)


def _pre_kernel(x_ref, cos_ref, sin_ref, ng_ref, wa_ref, wsb_ref, wmla_ref, qg_ref, wuq_ref, kvg_ref, wuk_ref,
                h_ref, qsb_ref, k_ref, v_ref, qlat_ref, qrope_ref, ckv_ref, kr_ref):
    ub = _rms(x_ref[...], ng_ref[...]).astype(BF16)
    ag = _dot(ub, wa_ref[...])
    h_ref[...] = ag[:, :CONV_CH] * _sigmoid(ag[:, CONV_CH:])
    sb = _dot(ub, wsb_ref[...])
    for hd in range(SB_HEADS):
        qsb_ref[hd] = (sb[:, LANES * hd:LANES * (hd + 1)] * (SB_HEAD_DIM ** -0.5)).astype(BF16)
    k_ref[...] = sb[:, SB_HEADS * LANES:SB_HEADS * LANES + SB_KV_WIDTH]
    v_ref[...] = sb[:, SB_HEADS * LANES + SB_KV_WIDTH:]
    m = _dot(ub, wmla_ref[...])
    cos = cos_ref[...]
    sin = sin_ref[...]
    ckv_ref[...] = _rms(m[:, MLA_Q_RANK:MLA_Q_RANK + MLA_KV_RANK], kvg_ref[...])
    o1 = MLA_Q_RANK + MLA_KV_RANK
    kr_ref[...] = (m[:, o1:o1 + MLA_ROPE_DIM] * cos[:, :MLA_ROPE_DIM]
                   + m[:, o1 + LANES:o1 + LANES + MLA_ROPE_DIM] * sin[:, :MLA_ROPE_DIM])
    cqn = _rms(m[:, :MLA_Q_RANK], qg_ref[...]).astype(BF16)
    qq = _dot(cqn, wuq_ref[...])
    nw = MLA_HEADS * MLA_NOPE_DIM
    rw = MLA_HEADS * MLA_ROPE_DIM
    rot = qq[:, nw:nw + rw] * cos + qq[:, nw + rw:] * sin
    for hd in range(MLA_HEADS):
        qrope_ref[hd] = rot[:, MLA_ROPE_DIM * hd:MLA_ROPE_DIM * (hd + 1)].astype(BF16)
        pair = hd // 2
        qn = qq[:, LANES * pair:LANES * (pair + 1)].astype(BF16)
        qlat_ref[hd] = _dot(qn, wuk_ref[hd]).astype(BF16)


def _pre_call(x, cos, sin, lw, tm):
    n = x.shape[0]
    nblk = cos.shape[0] // tm
    row = lambda i: (i, 0)
    hrow = lambda i: (0, i, 0)
    c2 = lambda i: (0, 0)
    c3 = lambda i: (0, 0, 0)
    tab = lambda i: (i % nblk, 0)
    rw = MLA_HEADS * MLA_ROPE_DIM
    in_specs = [
        pl.BlockSpec((tm, D_MODEL), row),
        pl.BlockSpec((tm, rw), tab),
        pl.BlockSpec((tm, rw), tab),
        pl.BlockSpec((1, D_MODEL), c2),
        pl.BlockSpec(lw['wa'].shape, c2),
        pl.BlockSpec(lw['wsb'].shape, c2),
        pl.BlockSpec(lw['wmla'].shape, c2),
        pl.BlockSpec((1, MLA_Q_RANK), c2),
        pl.BlockSpec(lw['wuq'].shape, c2),
        pl.BlockSpec((1, MLA_KV_RANK), c2),
        pl.BlockSpec(lw['wuk'].shape, c3),
    ]
    out_shape = [
        jax.ShapeDtypeStruct((n, CONV_CH), F32),
        jax.ShapeDtypeStruct((SB_HEADS, n, LANES), BF16),
        jax.ShapeDtypeStruct((n, SB_KV_WIDTH), F32),
        jax.ShapeDtypeStruct((n, SB_KV_WIDTH), F32),
        jax.ShapeDtypeStruct((MLA_HEADS, n, MLA_KV_RANK), BF16),
        jax.ShapeDtypeStruct((MLA_HEADS, n, MLA_ROPE_DIM), BF16),
        jax.ShapeDtypeStruct((n, MLA_KV_RANK), F32),
        jax.ShapeDtypeStruct((n, MLA_ROPE_DIM), F32),
    ]
    out_specs = [
        pl.BlockSpec((tm, CONV_CH), row),
        pl.BlockSpec((SB_HEADS, tm, LANES), hrow),
        pl.BlockSpec((tm, SB_KV_WIDTH), row),
        pl.BlockSpec((tm, SB_KV_WIDTH), row),
        pl.BlockSpec((MLA_HEADS, tm, MLA_KV_RANK), hrow),
        pl.BlockSpec((MLA_HEADS, tm, MLA_ROPE_DIM), hrow),
        pl.BlockSpec((tm, MLA_KV_RANK), row),
        pl.BlockSpec((tm, MLA_ROPE_DIM), row),
    ]
    return pl.pallas_call(
        _pre_kernel, grid=(n // tm,), in_specs=in_specs, out_specs=out_specs, out_shape=out_shape,
        compiler_params=_cparams(("parallel",)), name="pre",
    )(x, cos, sin, lw['norm_g'], lw['wa'], lw['wsb'], lw['wmla'], lw['qg'], lw['wuq'], lw['kvg'], lw['wuk'])


def _conv_kernel(h_ref, pre_ref, w_ref, b_ref, g_ref, beta_ref, o_ref, win_ref, y_ref, *, tt, rc):
    t = pl.program_id(1)

    @pl.when(t == 0)
    def _():
        win_ref[0:CONV_HALO, :] = pre_ref[0]

    win_ref[CONV_HALO:CONV_HALO + tt, :] = h_ref[0]
    shift = CONV_HALO - (CONV_WIDTH - 1)
    for r in range(tt // rc):
        for c in range(CONV_CH // LANES):
            cs = slice(c * LANES, (c + 1) * LANES)
            acc = jnp.broadcast_to(b_ref[:, cs], (rc, LANES))
            for k in range(CONV_WIDTH):
                lo = r * rc + shift + k
                acc = acc + w_ref[k:k + 1, cs] * win_ref[lo:lo + rc, cs]
            y_ref[r * rc:(r + 1) * rc, cs] = acc
    y = y_ref[...]
    mu = jnp.mean(y, axis=-1, keepdims=True)
    yc = y - mu
    ln = yc * lax.rsqrt(jnp.mean(yc * yc, axis=-1, keepdims=True) + NORM_EPS) * g_ref[...] + beta_ref[...]
    o_ref[0] = _silu(ln)
    win_ref[0:CONV_HALO, :] = win_ref[tt:tt + CONV_HALO, :]


def _conv_call(h, prefix, lw):
    b, t, _ = h.shape
    tt = min(256, t)
    rc = min(32, tt)
    c2 = lambda i, j: (0, 0)
    kern = functools.partial(_conv_kernel, tt=tt, rc=rc)
    return pl.pallas_call(
        kern, grid=(b, t // tt),
        in_specs=[
            pl.BlockSpec((1, tt, CONV_CH), lambda i, j: (i, j, 0)),
            pl.BlockSpec((1, CONV_HALO, CONV_CH), lambda i, j: (i, 0, 0)),
            pl.BlockSpec((CONV_WIDTH, CONV_CH), c2),
            pl.BlockSpec((1, CONV_CH), c2),
            pl.BlockSpec((1, CONV_CH), c2),
            pl.BlockSpec((1, CONV_CH), c2),
        ],
        out_specs=pl.BlockSpec((1, tt, CONV_CH), lambda i, j: (i, j, 0)),
        out_shape=jax.ShapeDtypeStruct((b, t, CONV_CH), F32),
        scratch_shapes=[pltpu.VMEM((tt + CONV_HALO, CONV_CH), F32), pltpu.VMEM((tt, CONV_CH), F32)],
        compiler_params=_cparams(("parallel", "arbitrary")), name="conv",
    )(h, prefix, lw['conv_w'], lw['conv_b'], lw['conv_ln_g'], lw['conv_ln_b'])


def _sb_logits(z):
    z2 = z * LOG2E
    return z2, jnp.maximum(z2, 0.0) + jnp.log2(1.0 + jnp.exp2(-jnp.abs(z2)))


def _split_bf16(x):
    hi = x.astype(BF16)
    return hi, (x - hi.astype(F32)).astype(BF16)


def _sb_tile(z, tri, run, mask):
    z2, sp = _sb_logits(z)
    if mask is not None:
        sp = jnp.where(mask, sp, 0.0)
    hi, lo = _split_bf16(sp)
    cum = _dot(hi, tri) + _dot(lo, tri)
    w = jnp.exp2(z2 - sp - cum - run)
    if mask is not None:
        w = jnp.where(mask, w, 0.0)
    return w, jnp.sum(sp, axis=1, keepdims=True)


def _tri_matrix(tk, ones_cols=0):
    j = np.arange(tk)[:, None]
    s = np.arange(tk + ones_cols)[None, :]
    return jnp.asarray(((j > s) | (s >= tk)).astype(np.float32), dtype=BF16)


def _sb_pair_store(o_ref, acc0, acc1, rows):
    lane = lax.broadcasted_iota(jnp.int32, (rows, LANES), 1)
    for p in range(SB_GROUP):
        o_ref[:, LANES * p:LANES * (p + 1)] = jnp.where(lane < SB_HEAD_DIM, acc0[p * rows:(p + 1) * rows],
                                                         acc1[p * rows:(p + 1) * rows])


def _sbp_kernel(qi_ref, kb_ref, first_ref, last_ref, q_ref, k_ref, v_ref, tri_ref, o_ref, acc_ref, run_ref,
                live_ref, *, tq):
    step = pl.program_id(1)
    rows = SB_GROUP * tq

    @pl.when(first_ref[step] == 1)
    def _():
        acc_ref[...] = jnp.zeros_like(acc_ref)
        run_ref[...] = jnp.zeros_like(run_ref)
        live_ref[0] = 1

    def body(masked):
        kb = k_ref[...].astype(BF16)
        vb = v_ref[...].astype(BF16)
        tri = tri_ref[...]
        mask = None
        if masked:
            tpos = lax.broadcasted_iota(jnp.int32, (SB_GROUP, tq, tq), 1).reshape(rows, tq)
            spos = lax.broadcasted_iota(jnp.int32, (rows, tq), 1)
            mask = spos < tpos
        run_min = None
        for g in range(SB_KV_HEADS):
            q = q_ref[SB_GROUP * g:SB_GROUP * (g + 1)].reshape(rows, LANES)
            z = _dot_nt(q, kb)
            run = run_ref[g]
            w, tot = _sb_tile(z, tri, _rep_lanes(run, tq), mask)
            acc_ref[g] += _dot(w.astype(BF16), vb)
            run = run + tot
            run_ref[g] = run
            run_min = run if run_min is None else jnp.minimum(run_min, run)
        live_ref[0] = (jnp.min(run_min) < SB_DEAD_LOG2).astype(jnp.int32)

    diag = qi_ref[step] == kb_ref[step]
    live = live_ref[0] == 1

    @pl.when(diag)
    def _():
        body(True)

    @pl.when(jnp.logical_and(jnp.logical_not(diag), live))
    def _():
        body(False)

    @pl.when(last_ref[step] == 1)
    def _():
        _sb_pair_store(o_ref, acc_ref[0], acc_ref[1], tq)


def _sbp_call(qsb, k, v, b, t):
    tq = min(256, t)
    nq = t // tq
    qi_l, kb_l, first_l, last_l = [], [], [], []
    for qi in range(nq):
        for kb in range(qi, -1, -1):
            qi_l.append(qi)
            kb_l.append(kb)
            first_l.append(int(kb == qi))
            last_l.append(int(kb == 0))
    tabs = [jnp.asarray(np.array(a, np.int32)) for a in (qi_l, kb_l, first_l, last_l)]
    n = b * t
    grid_spec = pltpu.PrefetchScalarGridSpec(
        num_scalar_prefetch=4, grid=(b, len(qi_l)),
        in_specs=[
            pl.BlockSpec((SB_HEADS, tq, LANES), lambda bi, s, qt, kt, ft, lt: (0, bi * nq + qt[s], 0)),
            pl.BlockSpec((tq, SB_KV_WIDTH), lambda bi, s, qt, kt, ft, lt: (bi * nq + kt[s], 0)),
            pl.BlockSpec((tq, SB_KV_WIDTH), lambda bi, s, qt, kt, ft, lt: (bi * nq + kt[s], 0)),
            pl.BlockSpec((tq, tq), lambda bi, s, qt, kt, ft, lt: (0, 0)),
        ],
        out_specs=pl.BlockSpec((tq, SB_WIDTH), lambda bi, s, qt, kt, ft, lt: (bi * nq + qt[s], 0)),
        scratch_shapes=[pltpu.VMEM((SB_KV_HEADS, SB_GROUP * tq, LANES), F32),
                        pltpu.VMEM((SB_KV_HEADS, SB_GROUP * tq, LANES), F32),
                        pltpu.SMEM((1,), jnp.int32)],
    )
    return pl.pallas_call(
        functools.partial(_sbp_kernel, tq=tq), grid_spec=grid_spec,
        out_shape=jax.ShapeDtypeStruct((n, SB_WIDTH), F32),
        compiler_params=_cparams(("parallel", "arbitrary")), name="sbp",
    )(*tabs, qsb, k, v, _tri_matrix(tq))


def _mla_update(s, cb, m_ref, l_ref, acc_ref):
    tk = s.shape[1]
    m_prev = m_ref[...]
    m_new = jnp.maximum(m_prev, jnp.max(s, axis=1, keepdims=True))
    p = jnp.exp2((s - _rep_lanes(m_new, tk)) * MLA_EXP2_SCALE)
    alpha = jnp.exp2((m_prev - m_new) * MLA_EXP2_SCALE)
    l_ref[...] = alpha * l_ref[...] + jnp.sum(p, axis=1, keepdims=True)
    acc_ref[...] = _rep_lanes(alpha, acc_ref.shape[1]) * acc_ref[...] + _dot(p.astype(BF16), cb)
    m_ref[...] = m_new


def _mla_finish(o_ref, wuv_ref, l_ref, acc_ref, rows):
    inv = _rep_lanes(1.0 / l_ref[...], acc_ref.shape[1])
    o = None
    for hd in range(MLA_HEADS):
        rs = slice(hd * rows, (hd + 1) * rows)
        oh = _dot((acc_ref[rs] * inv[rs]).astype(BF16), wuv_ref[hd])
        o = oh if o is None else o + oh
    o_ref[...] = o


def _mlap_kernel(qi_ref, kb_ref, first_ref, last_ref, ql_ref, qr_ref, c_ref, kr_ref, wuv_ref, o_ref,
                 m_ref, l_ref, acc_ref, *, tq, tk):
    step = pl.program_id(1)
    rows = MLA_HEADS * tq
    qi = qi_ref[step]
    kb = kb_ref[step]

    @pl.when(first_ref[step] == 1)
    def _():
        m_ref[...] = jnp.full_like(m_ref, NEG_BIG)
        l_ref[...] = jnp.zeros_like(l_ref)
        acc_ref[...] = jnp.zeros_like(acc_ref)

    def body(masked):
        cb = c_ref[...].astype(BF16)
        krb = kr_ref[...].astype(BF16)
        ql = ql_ref[...].reshape(rows, MLA_KV_RANK)
        qr = qr_ref[...].reshape(rows, MLA_ROPE_DIM)
        s = _dot_nt(ql, cb) + _dot_nt(qr, krb)
        if masked:
            tpos = qi * tq + lax.broadcasted_iota(jnp.int32, (MLA_HEADS, tq, tk), 1).reshape(rows, tk)
            spos = kb * tk + lax.broadcasted_iota(jnp.int32, (rows, tk), 1)
            s = jnp.where(spos <= tpos, s, NEG_BIG)
        _mla_update(s, cb, m_ref, l_ref, acc_ref)

    straddle = (kb + 1) * tk - 1 > qi * tq

    @pl.when(straddle)
    def _():
        body(True)

    @pl.when(jnp.logical_not(straddle))
    def _():
        body(False)

    @pl.when(last_ref[step] == 1)
    def _():
        _mla_finish(o_ref, wuv_ref, l_ref, acc_ref, tq)


def _mlap_call(qlat, qrope, ckv, kr, wuv, b, t):
    tq = min(128, t)
    tk = min(256, t)
    nq = t // tq
    nk = t // tk
    qi_l, kb_l, first_l, last_l = [], [], [], []
    for qi in range(nq):
        kmax = ((qi + 1) * tq - 1) // tk
        for kb in range(kmax + 1):
            qi_l.append(qi)
            kb_l.append(kb)
            first_l.append(int(kb == 0))
            last_l.append(int(kb == kmax))
    tabs = [jnp.asarray(np.array(a, np.int32)) for a in (qi_l, kb_l, first_l, last_l)]
    n = b * t
    rows = MLA_HEADS * tq
    grid_spec = pltpu.PrefetchScalarGridSpec(
        num_scalar_prefetch=4, grid=(b, len(qi_l)),
        in_specs=[
            pl.BlockSpec((MLA_HEADS, tq, MLA_KV_RANK), lambda bi, s, qt, kt, ft, lt: (0, bi * nq + qt[s], 0)),
            pl.BlockSpec((MLA_HEADS, tq, MLA_ROPE_DIM), lambda bi, s, qt, kt, ft, lt: (0, bi * nq + qt[s], 0)),
            pl.BlockSpec((tk, MLA_KV_RANK), lambda bi, s, qt, kt, ft, lt: (bi * nk + kt[s], 0)),
            pl.BlockSpec((tk, MLA_ROPE_DIM), lambda bi, s, qt, kt, ft, lt: (bi * nk + kt[s], 0)),
            pl.BlockSpec(wuv.shape, lambda bi, s, qt, kt, ft, lt: (0, 0, 0)),
        ],
        out_specs=pl.BlockSpec((tq, MLA_WIDTH), lambda bi, s, qt, kt, ft, lt: (bi * nq + qt[s], 0)),
        scratch_shapes=[pltpu.VMEM((rows, LANES), F32), pltpu.VMEM((rows, LANES), F32),
                        pltpu.VMEM((rows, MLA_KV_RANK), F32)],
    )
    return pl.pallas_call(
        functools.partial(_mlap_kernel, tq=tq, tk=tk), grid_spec=grid_spec,
        out_shape=jax.ShapeDtypeStruct((n, MLA_WIDTH), F32),
        compiler_params=_cparams(("parallel", "arbitrary")), name="mlap",
    )(*tabs, qlat, qrope, ckv, kr, wuv)


def _pad_new(ref, tn):
    x = ref[...]
    return jnp.concatenate([x, jnp.zeros((PAGE_SIZE - tn, x.shape[1]), F32)], axis=0).astype(BF16)


def _new_positions(rows, tn):
    tpos = lax.broadcasted_iota(jnp.int32, (rows // tn, tn, PAGE_SIZE), 1).reshape(rows, PAGE_SIZE)
    spos = lax.broadcasted_iota(jnp.int32, (rows, PAGE_SIZE), 1)
    return tpos, spos


def _page_slice(x, i):
    return x[:, i * PAGE_SIZE:(i + 1) * PAGE_SIZE]


def _ssb_kernel(pt_ref, q_ref, kn_ref, vn_ref, tri_ref, kt_hbm, vt_hbm, o_ref, kbuf, vbuf, sem, qs, acc_ref, run_ref,
                *, layer, npages, ppc, tn):
    s = pl.program_id(0)
    slot = s % 2
    rows = SB_HEADS * tn
    tc = ppc * PAGE_SIZE
    nchunks = npages // ppc

    def chunk_copies(seq, chunk, slot_):
        cps = []
        for i in range(ppc):
            page = pt_ref[seq, npages - ppc * (chunk + 1) + i]
            cps.append(pltpu.make_async_copy(kt_hbm.at[layer, page], kbuf.at[slot_, i], sem.at[slot_, 0]))
            cps.append(pltpu.make_async_copy(vt_hbm.at[layer, page], vbuf.at[slot_, i], sem.at[slot_, 1]))
        return cps

    @pl.when(s == 0)
    def _():
        for cp in chunk_copies(0, 0, 0):
            cp.start()

    @pl.when(s + 1 < pl.num_programs(0))
    def _():
        for cp in chunk_copies(s + 1, 0, 1 - slot):
            cp.start()

    qs[...] = q_ref[...].astype(F32).reshape(rows, LANES).astype(BF16)
    tpos, spos = _new_positions(rows, tn)
    z = _dot_nt(qs[...], _pad_new(kn_ref, tn))
    w, tot = _sb_tile(z, tri_ref[0:PAGE_SIZE, 0:PAGE_SIZE], jnp.zeros((rows, PAGE_SIZE), F32), spos < tpos)
    acc_ref[...] = _dot(w.astype(BF16), _pad_new(vn_ref, tn))
    run_ref[...] = jnp.broadcast_to(tot, (rows, LANES))

    def consume():
        q = qs[...]
        z = jnp.concatenate([_dot(q, kbuf[slot, i].astype(BF16)) for i in range(ppc)], axis=1)
        run = run_ref[...]
        w, tot = _sb_tile(z, tri_ref[...], _rep_lanes(run, tc), None)
        wb = w.astype(BF16)
        acc = acc_ref[...]
        for i in range(ppc):
            acc = acc + _dot_nt(_page_slice(wb, i), vbuf[slot, i].astype(BF16))
        acc_ref[...] = acc
        run = run + tot
        run_ref[...] = run
        return (jnp.min(run) < SB_DEAD_LOG2).astype(jnp.int32)

    for cp in chunk_copies(s, 0, slot):
        cp.wait()
    live0 = consume()

    def more(carry):
        chunk, live = carry
        return jnp.logical_and(chunk < nchunks, live == 1)

    def next_chunk(carry):
        chunk, _ = carry
        cps = chunk_copies(s, chunk, slot)
        for cp in cps:
            cp.start()
        for cp in cps:
            cp.wait()
        return chunk + 1, consume()

    lax.while_loop(more, next_chunk, (jnp.int32(1), live0))

    a = acc_ref[...]
    half = SB_GROUP * tn
    _sb_pair_store(o_ref, a[:half], a[half:], tn)


def _ssb_call(layer, page_table, qsb, kn, vn, cache_kt, cache_vt, tn):
    nseq, npages = page_table.shape
    ppc = min(4, npages)
    tc = ppc * PAGE_SIZE
    rows = SB_HEADS * tn
    tri = _tri_matrix(tc)
    page_shape = cache_kt.shape[2:]
    grid_spec = pltpu.PrefetchScalarGridSpec(
        num_scalar_prefetch=1, grid=(nseq,),
        in_specs=[
            pl.BlockSpec((SB_HEADS, tn, LANES), lambda s, pt: (0, s, 0)),
            pl.BlockSpec((tn, SB_KV_WIDTH), lambda s, pt: (s, 0)),
            pl.BlockSpec((tn, SB_KV_WIDTH), lambda s, pt: (s, 0)),
            pl.BlockSpec(tri.shape, lambda s, pt: (0, 0)),
            pl.BlockSpec(memory_space=pl.ANY),
            pl.BlockSpec(memory_space=pl.ANY),
        ],
        out_specs=pl.BlockSpec((tn, SB_WIDTH), lambda s, pt: (s, 0)),
        scratch_shapes=[
            pltpu.VMEM((2, ppc) + page_shape, F32), pltpu.VMEM((2, ppc) + page_shape, F32),
            pltpu.SemaphoreType.DMA((2, 2)),
            pltpu.VMEM((rows, LANES), BF16), pltpu.VMEM((rows, LANES), F32), pltpu.VMEM((rows, LANES), F32),
        ],
    )
    return pl.pallas_call(
        functools.partial(_ssb_kernel, layer=layer, npages=npages, ppc=ppc, tn=tn), grid_spec=grid_spec,
        out_shape=jax.ShapeDtypeStruct((nseq * tn, SB_WIDTH), F32),
        compiler_params=_cparams(("arbitrary",)), name="ssb",
    )(page_table, qsb, kn, vn, tri, cache_kt, cache_vt)


def _smla_kernel(pt_ref, ql_ref, qr_ref, cn_ref, krn_ref, wuv_ref, *rest, npg, tn):
    cs = rest[0:npg]
    rts = rest[npg:2 * npg]
    omla_ref = rest[2 * npg]
    ql_s, qr_s, m_ref, l_ref, acc_m = rest[2 * npg + 1:]
    j = pl.program_id(1)
    rows = MLA_HEADS * tn

    @pl.when(j == 0)
    def _():
        ql_s[...] = ql_ref[...].astype(F32).reshape(rows, MLA_KV_RANK).astype(BF16)
        qr_s[...] = qr_ref[...].astype(F32).reshape(rows, MLA_ROPE_DIM).astype(BF16)
        cnb = _pad_new(cn_ref, tn)
        tpos, spos = _new_positions(rows, tn)
        s = _dot_nt(ql_s[...], cnb) + _dot_nt(qr_s[...], _pad_new(krn_ref, tn))
        s = jnp.where(spos <= tpos, s, NEG_BIG)
        m_ref[...] = jnp.full_like(m_ref, NEG_BIG)
        l_ref[...] = jnp.zeros_like(l_ref)
        acc_m[...] = jnp.zeros_like(acc_m)
        _mla_update(s, cnb, m_ref, l_ref, acc_m)

    page_slice = _page_slice
    ql = ql_s[...]
    qr = qr_s[...]
    s = jnp.concatenate([_dot_nt(ql, cs[i][0, 0].astype(BF16)) + _dot(qr, rts[i][0, 0].astype(BF16))
                         for i in range(npg)], axis=1)
    m_prev = m_ref[...]
    m_new = jnp.maximum(m_prev, jnp.max(s, axis=1, keepdims=True))
    p = jnp.exp2((s - _rep_lanes(m_new, npg * PAGE_SIZE)) * MLA_EXP2_SCALE)
    alpha = jnp.exp2((m_prev - m_new) * MLA_EXP2_SCALE)
    l_ref[...] = alpha * l_ref[...] + jnp.sum(p, axis=1, keepdims=True)
    pb = p.astype(BF16)
    pv = None
    for i in range(npg):
        t = _dot(page_slice(pb, i), cs[i][0, 0].astype(BF16))
        pv = t if pv is None else pv + t
    acc_m[...] = _rep_lanes(alpha, MLA_KV_RANK) * acc_m[...] + pv
    m_ref[...] = m_new

    @pl.when(j == pl.num_programs(1) - 1)
    def _():
        _mla_finish(omla_ref, wuv_ref, l_ref, acc_m, tn)


def _smla_call(layer, page_table, qlat, qrope, cn, krn, wuv, cache_c, cache_rt, tn):
    nseq, npages = page_table.shape
    npg = min(16, npages)
    nsteps = npages // npg
    rows = MLA_HEADS * tn

    def seq3(s, j, pt):
        return (0, s, 0)

    def seq2(s, j, pt):
        return (s, 0)

    def page_map(i):
        return lambda s, j, pt: (layer, pt[s, npg * j + i], 0, 0)

    in_specs = [
        pl.BlockSpec((MLA_HEADS, tn, MLA_KV_RANK), seq3),
        pl.BlockSpec((MLA_HEADS, tn, MLA_ROPE_DIM), seq3),
        pl.BlockSpec((tn, MLA_KV_RANK), seq2),
        pl.BlockSpec((tn, MLA_ROPE_DIM), seq2),
        pl.BlockSpec(wuv.shape, lambda s, j, pt: (0, 0, 0)),
    ]
    operands = [qlat, qrope, cn, krn, wuv]
    for cache in (cache_c, cache_rt):
        for i in range(npg):
            in_specs.append(pl.BlockSpec((1, 1) + cache.shape[2:], page_map(i)))
            operands.append(cache)
    grid_spec = pltpu.PrefetchScalarGridSpec(
        num_scalar_prefetch=1, grid=(nseq, nsteps), in_specs=in_specs,
        out_specs=pl.BlockSpec((tn, MLA_WIDTH), seq2),
        scratch_shapes=[
            pltpu.VMEM((rows, MLA_KV_RANK), BF16), pltpu.VMEM((rows, MLA_ROPE_DIM), BF16),
            pltpu.VMEM((rows, LANES), F32), pltpu.VMEM((rows, LANES), F32),
            pltpu.VMEM((rows, MLA_KV_RANK), F32),
        ],
    )
    return pl.pallas_call(
        functools.partial(_smla_kernel, npg=npg, tn=tn), grid_spec=grid_spec,
        out_shape=jax.ShapeDtypeStruct((nseq * tn, MLA_WIDTH), F32),
        compiler_params=_cparams(("parallel", "arbitrary")), name="smla",
    )(page_table, *operands)


def _out_kernel(x_ref, brc_ref, osb_ref, omla_ref, ng_ref, wz_ref, wg_ref, woc_ref, wos_ref, wom_ref, wo_ref,
                fg_ref, o_ref, *, final):
    x = x_ref[...]
    ub = _rms(x, ng_ref[...]).astype(BF16)
    z = _dot(ub, wz_ref[...])
    merged = None
    branches = ((brc_ref, woc_ref, CONV_CH), (osb_ref, wos_ref, SB_WIDTH), (omla_ref, wom_ref, MLA_WIDTH))
    off = 0
    for i, (b_ref, w_ref, width) in enumerate(branches):
        br = (b_ref[...] * _silu(z[:, off:off + width])).astype(BF16)
        off += width
        gate = _sigmoid(_dot(ub, wg_ref[:, i * D_MODEL:(i + 1) * D_MODEL]))
        term = gate * _dot(br, w_ref[...])
        merged = term if merged is None else merged + term
    out = x + _dot(merged.astype(BF16), wo_ref[...])
    if final:
        out = _rms(out, fg_ref[...])
    o_ref[...] = out


def _out_call(x, brc, osb, omla, lw, final_g, final, tm):
    n = x.shape[0]
    row = lambda i: (i, 0)
    c2 = lambda i: (0, 0)
    wnames = ('wz', 'wg', 'woc', 'wos', 'wom', 'wo')
    in_specs = [pl.BlockSpec((tm, D_MODEL), row), pl.BlockSpec((tm, CONV_CH), row),
                pl.BlockSpec((tm, SB_WIDTH), row), pl.BlockSpec((tm, MLA_WIDTH), row),
                pl.BlockSpec((1, D_MODEL), c2)]
    in_specs += [pl.BlockSpec(lw[k].shape, c2) for k in wnames]
    in_specs += [pl.BlockSpec((1, D_MODEL), c2)]
    return pl.pallas_call(
        functools.partial(_out_kernel, final=final), grid=(n // tm,), in_specs=in_specs,
        out_specs=pl.BlockSpec((tm, D_MODEL), row), out_shape=jax.ShapeDtypeStruct((n, D_MODEL), F32),
        compiler_params=_cparams(("parallel",)), name="out",
    )(x, brc, osb, omla, lw['norm_g'], *[lw[k] for k in wnames], final_g)


def _layer_weights(l, norm_g, w_in, conv_w, conv_b, conv_ln_g, conv_ln_b, mla_q_norm_g, mla_w_uq, mla_kv_norm_g,
                   mla_w_uk, mla_w_uv, w_out_conv, w_out_sb, w_out_mla, w_o):
    wi = w_in[l]
    col = lambda i: wi[:, IN_OFF[i]:IN_OFF[i + 1]]
    half = MLA_ROPE_DIM // 2
    wq = col(3).reshape(D_MODEL, SB_HEADS, SB_HEAD_DIM)
    zq = jnp.zeros_like(wq)
    grp = (jnp.arange(SB_HEADS) // SB_GROUP)[None, :, None]
    wq_pad = jnp.where(grp == 0, jnp.concatenate([wq, zq], -1), jnp.concatenate([zq, wq], -1))
    wsb = jnp.concatenate([wq_pad.reshape(D_MODEL, SB_HEADS * LANES), col(4), col(5)], axis=1)
    wkr = col(9)
    wkr_sw = jnp.concatenate([-wkr[:, half:], wkr[:, :half]], axis=1)
    padk = jnp.zeros((D_MODEL, LANES - MLA_ROPE_DIM), F32)
    wmla = jnp.concatenate([col(7), col(8), wkr, padk, wkr_sw, padk], axis=1)
    wuq = mla_w_uq[l].reshape(MLA_Q_RANK, MLA_HEADS, MLA_NOPE_DIM + MLA_ROPE_DIM)
    wuq_n = wuq[:, :, :MLA_NOPE_DIM].reshape(MLA_Q_RANK, MLA_HEADS * MLA_NOPE_DIM)
    wuq_r = wuq[:, :, MLA_NOPE_DIM:]
    wuq_r1 = wuq_r.reshape(MLA_Q_RANK, MLA_HEADS * MLA_ROPE_DIM)
    wuq_r2 = jnp.concatenate([-wuq_r[:, :, half:], wuq_r[:, :, :half]], axis=-1).reshape(MLA_Q_RANK, -1)
    wuq_cat = jnp.concatenate([wuq_n, wuq_r1, wuq_r2], axis=1)
    ukt = jnp.transpose(mla_w_uk[l], (1, 2, 0))
    zk = jnp.zeros_like(ukt)
    odd = (jnp.arange(MLA_HEADS) % 2)[:, None, None]
    wuk = jnp.where(odd == 0, jnp.concatenate([ukt, zk], 1), jnp.concatenate([zk, ukt], 1))
    wuv2 = mla_w_uv[l].reshape(MLA_KV_RANK, MLA_WIDTH)
    hcol = (jnp.arange(MLA_WIDTH) // MLA_V_DIM)[None, None, :]
    wuv = jnp.where(hcol == jnp.arange(MLA_HEADS)[:, None, None], wuv2[None], 0.0)
    order = np.array(SB_HEAD_ORDER)
    wz_sb = col(6).reshape(D_MODEL, SB_HEADS, SB_HEAD_DIM)[:, order].reshape(D_MODEL, SB_WIDTH)
    wos = w_out_sb[l].reshape(SB_HEADS, SB_HEAD_DIM, D_MODEL)[order].reshape(SB_WIDTH, D_MODEL)
    bf = lambda a: a.astype(BF16)
    return {
        'norm_g': norm_g[l][None], 'wa': bf(jnp.concatenate([col(0), col(1)], axis=1)), 'wsb': bf(wsb),
        'wmla': bf(wmla), 'qg': mla_q_norm_g[l][None], 'wuq': bf(wuq_cat), 'kvg': mla_kv_norm_g[l][None],
        'wuk': bf(wuk), 'wuv': bf(wuv),
        'conv_w': conv_w[l], 'conv_b': conv_b[l][None], 'conv_ln_g': conv_ln_g[l][None],
        'conv_ln_b': conv_ln_b[l][None],
        'wz': bf(jnp.concatenate([col(2), wz_sb, col(10)], axis=1)), 'wg': bf(col(11)),
        'woc': bf(w_out_conv[l]), 'wos': bf(wos), 'wom': bf(w_out_mla[l]), 'wo': bf(w_o[l]),
    }


def _rope_tables(pos):
    inv = ROPE_THETA ** (-jnp.arange(0, MLA_ROPE_DIM, 2, dtype=F32) / MLA_ROPE_DIM)
    ang = pos.astype(F32)[:, None] * inv[None, :]
    cos = jnp.cos(ang)
    sin = jnp.sin(ang)
    return (jnp.tile(jnp.concatenate([cos, cos], axis=1), (1, MLA_HEADS)),
            jnp.tile(jnp.concatenate([sin, sin], axis=1), (1, MLA_HEADS)))


def kernel(x_prompt, x_sample, cache_sb_k, cache_sb_v, cache_mla_latent, cache_mla_rope, state_conv, page_table,
           norm_g, w_in, conv_w, conv_b, conv_ln_g, conv_ln_b, mla_q_norm_g, mla_w_uq, mla_kv_norm_g, mla_w_uk,
           mla_w_uv, w_out_conv, w_out_sb, w_out_mla, w_o, final_norm_g):
    b_p, seq, _ = x_prompt.shape
    b_s, t_new, _ = x_sample.shape
    depth = norm_g.shape[0]
    n_pool = cache_sb_k.shape[1]
    past_len = page_table.shape[1] * PAGE_SIZE
    n_p = b_p * seq
    n_s = b_s * t_new
    tm_p = min(256, n_p)
    tm_s = min(256, n_s)

    cos_p, sin_p = _rope_tables(jnp.arange(seq, dtype=jnp.int32))
    pos_s = past_len + (jnp.arange(tm_s, dtype=jnp.int32) % t_new)
    cos_s, sin_s = _rope_tables(pos_s)
    cache_kt = jnp.transpose(cache_sb_k, (0, 1, 3, 4, 2)).reshape(depth, n_pool, SB_KV_WIDTH, PAGE_SIZE)
    cache_vt = jnp.transpose(cache_sb_v, (0, 1, 3, 4, 2)).reshape(depth, n_pool, SB_KV_WIDTH, PAGE_SIZE)
    cache_rt = jnp.transpose(cache_mla_rope, (0, 1, 3, 2))
    fg = final_norm_g[None]
    zero_prefix = jnp.zeros((b_p, CONV_HALO, CONV_CH), F32)
    halo_pad = CONV_HALO - (CONV_WIDTH - 1)

    xp = x_prompt.reshape(n_p, D_MODEL)
    xs = x_sample.reshape(n_s, D_MODEL)
    st_p, st_s = [], []
    for l in range(depth):
        lw = _layer_weights(l, norm_g, w_in, conv_w, conv_b, conv_ln_g, conv_ln_b, mla_q_norm_g, mla_w_uq,
                            mla_kv_norm_g, mla_w_uk, mla_w_uv, w_out_conv, w_out_sb, w_out_mla, w_o)
        final = l == depth - 1
        h, qsb, k, v, qlat, qrope, ckv, kr = _pre_call(xp, cos_p, sin_p, lw, tm_p)
        h3 = h.reshape(b_p, seq, CONV_CH)
        brc = _conv_call(h3, zero_prefix, lw).reshape(n_p, CONV_CH)
        osb = _sbp_call(qsb, k, v, b_p, seq)
        omla = _mlap_call(qlat, qrope, ckv, kr, lw['wuv'], b_p, seq)
        xp = _out_call(xp, brc, osb, omla, lw, fg, final, tm_p)
        hist_p = jnp.concatenate([jnp.zeros((b_p, CONV_WIDTH - 1, CONV_CH), F32), h3], axis=1)
        st_p.append((k.reshape(b_p, seq, SB_KV_HEADS, SB_HEAD_DIM), v.reshape(b_p, seq, SB_KV_HEADS, SB_HEAD_DIM),
                     ckv.reshape(b_p, seq, MLA_KV_RANK), kr.reshape(b_p, seq, MLA_ROPE_DIM),
                     hist_p[:, -(CONV_WIDTH - 1):]))
        h, qsb, k, v, qlat, qrope, ckv, kr = _pre_call(xs, cos_s, sin_s, lw, tm_s)
        h3 = h.reshape(b_s, t_new, CONV_CH)
        prefix = jnp.pad(state_conv[l], ((0, 0), (halo_pad, 0), (0, 0)))
        brc = _conv_call(h3, prefix, lw).reshape(n_s, CONV_CH)
        osb = _ssb_call(l, page_table, qsb, k, v, cache_kt, cache_vt, t_new)
        omla = _smla_call(l, page_table, qlat, qrope, ckv, kr, lw['wuv'], cache_mla_latent, cache_rt, t_new)
        xs = _out_call(xs, brc, osb, omla, lw, fg, final, tm_s)
        hist_s = jnp.concatenate([state_conv[l], h3], axis=1)
        st_s.append((k.reshape(b_s, t_new, SB_KV_HEADS, SB_HEAD_DIM), v.reshape(b_s, t_new, SB_KV_HEADS, SB_HEAD_DIM),
                     ckv.reshape(b_s, t_new, MLA_KV_RANK), kr.reshape(b_s, t_new, MLA_ROPE_DIM),
                     hist_s[:, -(CONV_WIDTH - 1):]))

    outs = [xp.reshape(b_p, seq, D_MODEL), xs.reshape(b_s, t_new, D_MODEL)]
    for st in (st_p, st_s):
        for i in range(5):
            outs.append(jnp.stack([s[i] for s in st]))
    return tuple(outs)
```

```python
import functools

import numpy as np
import jax
import jax.numpy as jnp
from jax import lax
from jax.experimental import pallas as pl
from jax.experimental.pallas import tpu as pltpu

F32 = jnp.float32
BF16 = jnp.bfloat16

D_MODEL = 1024
PAGE_SIZE = 128
CONV_CH = 512
CONV_WIDTH = 31
SB_HEADS = 8
SB_KV_HEADS = 2
SB_GROUP = SB_HEADS // SB_KV_HEADS
SB_HEAD_DIM = 64
SB_WIDTH = SB_HEADS * SB_HEAD_DIM
SB_KV_WIDTH = SB_KV_HEADS * SB_HEAD_DIM
MLA_HEADS = 8
MLA_Q_RANK = 256
MLA_KV_RANK = 256
MLA_NOPE_DIM = 64
MLA_ROPE_DIM = 32
MLA_V_DIM = 64
MLA_WIDTH = MLA_HEADS * MLA_V_DIM
MLA_SCALE = (MLA_NOPE_DIM + MLA_ROPE_DIM) ** -0.5
ROPE_THETA = 10000.0
N_BRANCHES = 3
NORM_EPS = 1e-6
IN_SIZES = (CONV_CH, CONV_CH, CONV_CH, SB_WIDTH, SB_KV_WIDTH, SB_KV_WIDTH, SB_WIDTH, MLA_Q_RANK, MLA_KV_RANK,
            MLA_ROPE_DIM, MLA_WIDTH, N_BRANCHES * D_MODEL)
IN_OFF = tuple(int(v) for v in np.concatenate([[0], np.cumsum(IN_SIZES)]))

LANES = 128
CONV_HALO = 32
LOG2E = 1.4426950408889634
MLA_EXP2_SCALE = MLA_SCALE * LOG2E
NEG_BIG = -1e30
SB_DEAD_LOG2 = 150.0
VMEM_LIMIT = 48 * 1024 * 1024
SB_HEAD_ORDER = tuple(h for p in range(SB_GROUP) for h in (p, SB_GROUP + p))


def _dot(a, b):
    return jnp.dot(a, b, preferred_element_type=F32)


def _dot_nt(a, b):
    return lax.dot_general(a, b, (((1,), (1,)), ((), ())), preferred_element_type=F32)


def _sigmoid(x):
    return 1.0 / (1.0 + jnp.exp(-x))


def _silu(x):
    return x * _sigmoid(x)


def _rms(x, g):
    return x * lax.rsqrt(jnp.mean(x * x, axis=-1, keepdims=True) + NORM_EPS) * g


def _rep_lanes(x, width):
    reps = width // LANES
    return x if reps == 1 else jnp.concatenate([x] * reps, axis=1)


def _cparams(sem):
    return pltpu.CompilerParams(dimension_semantics=sem, vmem_limit_bytes=VMEM_LIMIT)


def _pre_kernel(x_ref, cos_ref, sin_ref, ng_ref, wa_ref, wsb_ref, wmla_ref, qg_ref, wuq_ref, kvg_ref, wuk_ref,
                h_ref, qsb_ref, k_ref, v_ref, qlat_ref, qrope_ref, ckv_ref, kr_ref, kb_ref, vb_ref, ckvb_ref, krb_ref):
    ub = _rms(x_ref[...], ng_ref[...]).astype(BF16)
    ag = _dot(ub, wa_ref[...])
    h_ref[...] = ag[:, :CONV_CH] * _sigmoid(ag[:, CONV_CH:])
    sb = _dot(ub, wsb_ref[...])
    for hd in range(SB_HEADS):
        qsb_ref[hd] = (sb[:, LANES * hd:LANES * (hd + 1)] * (SB_HEAD_DIM ** -0.5)).astype(BF16)
    k = sb[:, SB_HEADS * LANES:SB_HEADS * LANES + SB_KV_WIDTH]
    v = sb[:, SB_HEADS * LANES + SB_KV_WIDTH:]
    k_ref[...] = k
    v_ref[...] = v
    kb_ref[...] = k.astype(BF16)
    vb_ref[...] = v.astype(BF16)
    m = _dot(ub, wmla_ref[...])
    cos = cos_ref[...]
    sin = sin_ref[...]
    ckv = _rms(m[:, MLA_Q_RANK:MLA_Q_RANK + MLA_KV_RANK], kvg_ref[...])
    o1 = MLA_Q_RANK + MLA_KV_RANK
    kr = (m[:, o1:o1 + MLA_ROPE_DIM] * cos[:, :MLA_ROPE_DIM]
          + m[:, o1 + LANES:o1 + LANES + MLA_ROPE_DIM] * sin[:, :MLA_ROPE_DIM])
    ckv_ref[...] = ckv
    kr_ref[...] = kr
    ckvb_ref[...] = ckv.astype(BF16)
    krb_ref[...] = kr.astype(BF16)
    cqn = _rms(m[:, :MLA_Q_RANK], qg_ref[...]).astype(BF16)
    qq = _dot(cqn, wuq_ref[...])
    nw = MLA_HEADS * MLA_NOPE_DIM
    rw = MLA_HEADS * MLA_ROPE_DIM
    rot = qq[:, nw:nw + rw] * cos + qq[:, nw + rw:] * sin
    for hd in range(MLA_HEADS):
        qrope_ref[hd] = rot[:, MLA_ROPE_DIM * hd:MLA_ROPE_DIM * (hd + 1)].astype(BF16)
        pair = hd // 2
        qn = qq[:, LANES * pair:LANES * (pair + 1)].astype(BF16)
        qlat_ref[hd] = _dot(qn, wuk_ref[hd]).astype(BF16)


def _pre_call(x, cos, sin, lw, tm):
    n = x.shape[0]
    nblk = cos.shape[0] // tm
    row = lambda i: (i, 0)
    hrow = lambda i: (0, i, 0)
    c2 = lambda i: (0, 0)
    c3 = lambda i: (0, 0, 0)
    tab = lambda i: (i % nblk, 0)
    rw = MLA_HEADS * MLA_ROPE_DIM
    in_specs = [
        pl.BlockSpec((tm, D_MODEL), row),
        pl.BlockSpec((tm, rw), tab),
        pl.BlockSpec((tm, rw), tab),
        pl.BlockSpec((1, D_MODEL), c2),
        pl.BlockSpec(lw['wa'].shape, c2),
        pl.BlockSpec(lw['wsb'].shape, c2),
        pl.BlockSpec(lw['wmla'].shape, c2),
        pl.BlockSpec((1, MLA_Q_RANK), c2),
        pl.BlockSpec(lw['wuq'].shape, c2),
        pl.BlockSpec((1, MLA_KV_RANK), c2),
        pl.BlockSpec(lw['wuk'].shape, c3),
    ]
    out_shape = [
        jax.ShapeDtypeStruct((n, CONV_CH), F32),
        jax.ShapeDtypeStruct((SB_HEADS, n, LANES), BF16),
        jax.ShapeDtypeStruct((n, SB_KV_WIDTH), F32),
        jax.ShapeDtypeStruct((n, SB_KV_WIDTH), F32),
        jax.ShapeDtypeStruct((MLA_HEADS, n, MLA_KV_RANK), BF16),
        jax.ShapeDtypeStruct((MLA_HEADS, n, MLA_ROPE_DIM), BF16),
        jax.ShapeDtypeStruct((n, MLA_KV_RANK), F32),
        jax.ShapeDtypeStruct((n, MLA_ROPE_DIM), F32),
        jax.ShapeDtypeStruct((n, SB_KV_WIDTH), BF16),
        jax.ShapeDtypeStruct((n, SB_KV_WIDTH), BF16),
        jax.ShapeDtypeStruct((n, MLA_KV_RANK), BF16),
        jax.ShapeDtypeStruct((n, MLA_ROPE_DIM), BF16),
    ]
    out_specs = [
        pl.BlockSpec((tm, CONV_CH), row),
        pl.BlockSpec((SB_HEADS, tm, LANES), hrow),
        pl.BlockSpec((tm, SB_KV_WIDTH), row),
        pl.BlockSpec((tm, SB_KV_WIDTH), row),
        pl.BlockSpec((MLA_HEADS, tm, MLA_KV_RANK), hrow),
        pl.BlockSpec((MLA_HEADS, tm, MLA_ROPE_DIM), hrow),
        pl.BlockSpec((tm, MLA_KV_RANK), row),
        pl.BlockSpec((tm, MLA_ROPE_DIM), row),
        pl.BlockSpec((tm, SB_KV_WIDTH), row),
        pl.BlockSpec((tm, SB_KV_WIDTH), row),
        pl.BlockSpec((tm, MLA_KV_RANK), row),
        pl.BlockSpec((tm, MLA_ROPE_DIM), row),
    ]
    return pl.pallas_call(
        _pre_kernel, grid=(n // tm,), in_specs=in_specs, out_specs=out_specs, out_shape=out_shape,
        compiler_params=_cparams(("parallel",)), name="pre",
    )(x, cos, sin, lw['norm_g'], lw['wa'], lw['wsb'], lw['wmla'], lw['qg'], lw['wuq'], lw['kvg'], lw['wuk'])


def _conv_kernel(h_ref, pre_ref, w_ref, b_ref, g_ref, beta_ref, o_ref, win_ref, y_ref, *, tt, rc):
    t = pl.program_id(1)

    @pl.when(t == 0)
    def _():
        win_ref[0:CONV_HALO, :] = pre_ref[0]

    win_ref[CONV_HALO:CONV_HALO + tt, :] = h_ref[0]
    shift = CONV_HALO - (CONV_WIDTH - 1)
    for r in range(tt // rc):
        for c in range(CONV_CH // LANES):
            cs = slice(c * LANES, (c + 1) * LANES)
            acc = jnp.broadcast_to(b_ref[:, cs], (rc, LANES))
            for k in range(CONV_WIDTH):
                lo = r * rc + shift + k
                acc = acc + w_ref[k:k + 1, cs] * win_ref[lo:lo + rc, cs]
            y_ref[r * rc:(r + 1) * rc, cs] = acc
    y = y_ref[...]
    mu = jnp.mean(y, axis=-1, keepdims=True)
    yc = y - mu
    ln = yc * lax.rsqrt(jnp.mean(yc * yc, axis=-1, keepdims=True) + NORM_EPS) * g_ref[...] + beta_ref[...]
    o_ref[0] = _silu(ln)
    win_ref[0:CONV_HALO, :] = win_ref[tt:tt + CONV_HALO, :]


def _conv_call(h, prefix, lw):
    b, t, _ = h.shape
    tt = min(256, t)
    rc = min(32, tt)
    c2 = lambda i, j: (0, 0)
    kern = functools.partial(_conv_kernel, tt=tt, rc=rc)
    return pl.pallas_call(
        kern, grid=(b, t // tt),
        in_specs=[
            pl.BlockSpec((1, tt, CONV_CH), lambda i, j: (i, j, 0)),
            pl.BlockSpec((1, CONV_HALO, CONV_CH), lambda i, j: (i, 0, 0)),
            pl.BlockSpec((CONV_WIDTH, CONV_CH), c2),
            pl.BlockSpec((1, CONV_CH), c2),
            pl.BlockSpec((1, CONV_CH), c2),
            pl.BlockSpec((1, CONV_CH), c2),
        ],
        out_specs=pl.BlockSpec((1, tt, CONV_CH), lambda i, j: (i, j, 0)),
        out_shape=jax.ShapeDtypeStruct((b, t, CONV_CH), F32),
        scratch_shapes=[pltpu.VMEM((tt + CONV_HALO, CONV_CH), F32), pltpu.VMEM((tt, CONV_CH), F32)],
        compiler_params=_cparams(("parallel", "arbitrary")), name="conv",
    )(h, prefix, lw['conv_w'], lw['conv_b'], lw['conv_ln_g'], lw['conv_ln_b'])


def _sb_logits(z):
    z2 = z * LOG2E
    return z2, jnp.maximum(z2, 0.0) + jnp.log2(1.0 + jnp.exp2(-jnp.abs(z2)))


def _split_bf16(x):
    hi = x.astype(BF16)
    return hi, (x - hi.astype(F32)).astype(BF16)


def _sb_tile(z, tri, run, mask):
    z2, sp = _sb_logits(z)
    if mask is not None:
        sp = jnp.where(mask, sp, 0.0)
    hi, lo = _split_bf16(sp)
    cum = _dot(hi, tri) + _dot(lo, tri)
    w = jnp.exp2(z2 - sp - cum - run)
    if mask is not None:
        w = jnp.where(mask, w, 0.0)
    return w, jnp.sum(sp, axis=1, keepdims=True)


def _tri_matrix(tk, ones_cols=0):
    j = np.arange(tk)[:, None]
    s = np.arange(tk + ones_cols)[None, :]
    return jnp.asarray(((j > s) | (s >= tk)).astype(np.float32), dtype=BF16)


def _sb_pair_store(o_ref, acc0, acc1, rows):
    lane = lax.broadcasted_iota(jnp.int32, (rows, LANES), 1)
    for p in range(SB_GROUP):
        o_ref[:, LANES * p:LANES * (p + 1)] = jnp.where(lane < SB_HEAD_DIM, acc0[p * rows:(p + 1) * rows],
                                                         acc1[p * rows:(p + 1) * rows])


def _sbp_kernel(q_ref, k_ref, v_ref, tri_ref, o_ref, acc_ref, run_ref, *, tq):
    qi = pl.program_id(1)
    rows = SB_GROUP * tq
    acc_ref[...] = jnp.zeros_like(acc_ref)
    run_ref[...] = jnp.zeros_like(run_ref)

    def block(kblk, masked):
        off = pl.multiple_of(kblk * tq, tq)
        kb = k_ref[pl.ds(off, tq), :]
        vb = v_ref[pl.ds(off, tq), :]
        tri = tri_ref[...]
        mask = None
        if masked:
            tpos = lax.broadcasted_iota(jnp.int32, (SB_GROUP, tq, tq), 1).reshape(rows, tq)
            spos = lax.broadcasted_iota(jnp.int32, (rows, tq), 1)
            mask = spos < tpos
        run_min = None
        for g in range(SB_KV_HEADS):
            q = q_ref[SB_GROUP * g:SB_GROUP * (g + 1)].reshape(rows, LANES)
            z = _dot_nt(q, kb)
            run = run_ref[g]
            w, tot = _sb_tile(z, tri, _rep_lanes(run, tq), mask)
            acc_ref[g] += _dot(w.astype(BF16), vb)
            run = run + tot
            run_ref[g] = run
            run_min = run if run_min is None else jnp.minimum(run_min, run)
        return (jnp.min(run_min) < SB_DEAD_LOG2).astype(jnp.int32)

    def more(carry):
        kblk, live = carry
        return jnp.logical_and(kblk >= 0, live == 1)

    def step(carry):
        kblk, _ = carry
        return kblk - 1, block(kblk, False)

    lax.while_loop(more, step, (qi - 1, block(qi, True)))
    _sb_pair_store(o_ref, acc_ref[0], acc_ref[1], tq)


def _sbp_call(qsb, kb16, vb16, b, t):
    tq = min(256, t)
    nq = t // tq
    n = b * t
    return pl.pallas_call(
        functools.partial(_sbp_kernel, tq=tq), grid=(b, nq),
        in_specs=[
            pl.BlockSpec((SB_HEADS, tq, LANES), lambda bi, qi: (0, bi * nq + qi, 0)),
            pl.BlockSpec((t, SB_KV_WIDTH), lambda bi, qi: (bi, 0)),
            pl.BlockSpec((t, SB_KV_WIDTH), lambda bi, qi: (bi, 0)),
            pl.BlockSpec((tq, tq), lambda bi, qi: (0, 0)),
        ],
        out_specs=pl.BlockSpec((tq, SB_WIDTH), lambda bi, qi: (bi * nq + qi, 0)),
        out_shape=jax.ShapeDtypeStruct((n, SB_WIDTH), F32),
        scratch_shapes=[pltpu.VMEM((SB_KV_HEADS, SB_GROUP * tq, LANES), F32),
                        pltpu.VMEM((SB_KV_HEADS, SB_GROUP * tq, LANES), F32)],
        compiler_params=_cparams(("parallel", "arbitrary")), name="sbp",
    )(qsb, kb16, vb16, _tri_matrix(tq))


def _mla_update(s, cb, m_ref, l_ref, acc_ref):
    tk = s.shape[1]
    m_prev = m_ref[...]
    m_new = jnp.maximum(m_prev, jnp.max(s, axis=1, keepdims=True))
    p = jnp.exp2((s - _rep_lanes(m_new, tk)) * MLA_EXP2_SCALE)
    alpha = jnp.exp2((m_prev - m_new) * MLA_EXP2_SCALE)
    l_ref[...] = alpha * l_ref[...] + jnp.sum(p, axis=1, keepdims=True)
    acc_ref[...] = _rep_lanes(alpha, acc_ref.shape[1]) * acc_ref[...] + _dot(p.astype(BF16), cb)
    m_ref[...] = m_new


def _mla_finish(o_ref, wuv_ref, l_ref, acc_ref, rows):
    inv = _rep_lanes(1.0 / l_ref[...], acc_ref.shape[1])
    o = None
    for hd in range(MLA_HEADS):
        rs = slice(hd * rows, (hd + 1) * rows)
        oh = _dot((acc_ref[rs] * inv[rs]).astype(BF16), wuv_ref[hd])
        o = oh if o is None else o + oh
    o_ref[...] = o


def _mlap_kernel(ql_ref, qr_ref, c_ref, kr_ref, wuv_ref, o_ref, s_ref, m_ref, l_ref, acc_ref, *, tq, tk):
    qi = pl.program_id(1)
    rows = MLA_HEADS * tq
    ql = ql_ref[...].reshape(rows, MLA_KV_RANK)
    qr = qr_ref[...].reshape(rows, MLA_ROPE_DIM)
    m_ref[...] = jnp.full_like(m_ref, NEG_BIG)
    l_ref[...] = jnp.zeros_like(l_ref)
    acc_ref[...] = jnp.zeros_like(acc_ref)

    def rows_of(kblk):
        return pl.ds(pl.multiple_of(kblk * tk, tk), tk)

    def scores(kblk):
        return _dot_nt(ql, c_ref[rows_of(kblk), :]) + _dot_nt(qr, kr_ref[rows_of(kblk), :])

    def update(slot, kblk, masked):
        s = s_ref[slot]
        if masked:
            tpos = qi * tq + lax.broadcasted_iota(jnp.int32, (MLA_HEADS, tq, tk), 1).reshape(rows, tk)
            spos = kblk * tk + lax.broadcasted_iota(jnp.int32, (rows, tk), 1)
            s = jnp.where(spos <= tpos, s, NEG_BIG)
        _mla_update(s, c_ref[rows_of(kblk), :], m_ref, l_ref, acc_ref)

    n_full = (qi * tq + 1) // tk
    s_ref[0] = scores(0)

    def pair(jj, carry):
        j = 2 * jj
        s_ref[1] = scores(j + 1)
        update(0, j, False)
        s_ref[0] = scores(j + 2)
        update(1, j + 1, False)
        return carry

    lax.fori_loop(0, n_full // 2, pair, 0)
    odd = n_full % 2 == 1

    @pl.when(odd)
    def _():
        s_ref[1] = scores(n_full)
        update(0, n_full - 1, False)
        update(1, n_full, True)

    @pl.when(jnp.logical_not(odd))
    def _():
        update(0, n_full, True)

    _mla_finish(o_ref, wuv_ref, l_ref, acc_ref, tq)


def _mlap_call(qlat, qrope, ckvb16, krb16, wuv, b, t):
    tq = min(128, t)
    tk = min(256, t)
    assert tk % tq == 0
    nq = t // tq
    n = b * t
    rows = MLA_HEADS * tq
    return pl.pallas_call(
        functools.partial(_mlap_kernel, tq=tq, tk=tk), grid=(b, nq),
        in_specs=[
            pl.BlockSpec((MLA_HEADS, tq, MLA_KV_RANK), lambda bi, qi: (0, bi * nq + qi, 0)),
            pl.BlockSpec((MLA_HEADS, tq, MLA_ROPE_DIM), lambda bi, qi: (0, bi * nq + qi, 0)),
            pl.BlockSpec((t, MLA_KV_RANK), lambda bi, qi: (bi, 0)),
            pl.BlockSpec((t, MLA_ROPE_DIM), lambda bi, qi: (bi, 0)),
            pl.BlockSpec(wuv.shape, lambda bi, qi: (0, 0, 0)),
        ],
        out_specs=pl.BlockSpec((tq, MLA_WIDTH), lambda bi, qi: (bi * nq + qi, 0)),
        out_shape=jax.ShapeDtypeStruct((n, MLA_WIDTH), F32),
        scratch_shapes=[pltpu.VMEM((2, rows, tk), F32), pltpu.VMEM((rows, LANES), F32),
                        pltpu.VMEM((rows, LANES), F32), pltpu.VMEM((rows, MLA_KV_RANK), F32)],
        compiler_params=_cparams(("parallel", "arbitrary")), name="mlap",
    )(qlat, qrope, ckvb16, krb16, wuv)


def _pad_new(ref, tn):
    x = ref[...]
    return jnp.concatenate([x, jnp.zeros((PAGE_SIZE - tn, x.shape[1]), F32)], axis=0).astype(BF16)


def _new_positions(rows, tn):
    tpos = lax.broadcasted_iota(jnp.int32, (rows // tn, tn, PAGE_SIZE), 1).reshape(rows, PAGE_SIZE)
    spos = lax.broadcasted_iota(jnp.int32, (rows, PAGE_SIZE), 1)
    return tpos, spos


def _page_slice(x, i):
    return x[:, i * PAGE_SIZE:(i + 1) * PAGE_SIZE]


def _ssb_kernel(pt_ref, q_ref, kn_ref, vn_ref, tri_ref, kt_hbm, vt_hbm, o_ref, kbuf, vbuf, sem, qs, acc_ref, run_ref,
                *, layer, npages, ppc, tn):
    s = pl.program_id(0)
    slot = s % 2
    rows = SB_HEADS * tn
    tc = ppc * PAGE_SIZE
    nchunks = npages // ppc

    def chunk_copies(seq, chunk, slot_):
        cps = []
        for i in range(ppc):
            page = pt_ref[seq, npages - ppc * (chunk + 1) + i]
            cps.append(pltpu.make_async_copy(kt_hbm.at[layer, page], kbuf.at[slot_, i], sem.at[slot_, 0]))
            cps.append(pltpu.make_async_copy(vt_hbm.at[layer, page], vbuf.at[slot_, i], sem.at[slot_, 1]))
        return cps

    @pl.when(s == 0)
    def _():
        for cp in chunk_copies(0, 0, 0):
            cp.start()

    @pl.when(s + 1 < pl.num_programs(0))
    def _():
        for cp in chunk_copies(s + 1, 0, 1 - slot):
            cp.start()

    qs[...] = q_ref[...].astype(F32).reshape(rows, LANES).astype(BF16)
    tpos, spos = _new_positions(rows, tn)
    z = _dot_nt(qs[...], _pad_new(kn_ref, tn))
    w, tot = _sb_tile(z, tri_ref[0:PAGE_SIZE, 0:PAGE_SIZE], jnp.zeros((rows, PAGE_SIZE), F32), spos < tpos)
    acc_ref[...] = _dot(w.astype(BF16), _pad_new(vn_ref, tn))
    run_ref[...] = jnp.broadcast_to(tot, (rows, LANES))

    def consume():
        q = qs[...]
        z = jnp.concatenate([_dot(q, kbuf[slot, i].astype(BF16)) for i in range(ppc)], axis=1)
        run = run_ref[...]
        w, tot = _sb_tile(z, tri_ref[...], _rep_lanes(run, tc), None)
        wb = w.astype(BF16)
        acc = acc_ref[...]
        for i in range(ppc):
            acc = acc + _dot_nt(_page_slice(wb, i), vbuf[slot, i].astype(BF16))
        acc_ref[...] = acc
        run = run + tot
        run_ref[...] = run
        return (jnp.min(run) < SB_DEAD_LOG2).astype(jnp.int32)

    for cp in chunk_copies(s, 0, slot):
        cp.wait()
    live0 = consume()

    def more(carry):
        chunk, live = carry
        return jnp.logical_and(chunk < nchunks, live == 1)

    def next_chunk(carry):
        chunk, _ = carry
        cps = chunk_copies(s, chunk, slot)
        for cp in cps:
            cp.start()
        for cp in cps:
            cp.wait()
        return chunk + 1, consume()

    lax.while_loop(more, next_chunk, (jnp.int32(1), live0))

    a = acc_ref[...]
    half = SB_GROUP * tn
    _sb_pair_store(o_ref, a[:half], a[half:], tn)


def _ssb_call(layer, page_table, qsb, kn, vn, cache_kt, cache_vt, tn):
    nseq, npages = page_table.shape
    ppc = min(4, npages)
    tc = ppc * PAGE_SIZE
    rows = SB_HEADS * tn
    tri = _tri_matrix(tc)
    page_shape = cache_kt.shape[2:]
    grid_spec = pltpu.PrefetchScalarGridSpec(
        num_scalar_prefetch=1, grid=(nseq,),
        in_specs=[
            pl.BlockSpec((SB_HEADS, tn, LANES), lambda s, pt: (0, s, 0)),
            pl.BlockSpec((tn, SB_KV_WIDTH), lambda s, pt: (s, 0)),
            pl.BlockSpec((tn, SB_KV_WIDTH), lambda s, pt: (s, 0)),
            pl.BlockSpec(tri.shape, lambda s, pt: (0, 0)),
            pl.BlockSpec(memory_space=pl.ANY),
            pl.BlockSpec(memory_space=pl.ANY),
        ],
        out_specs=pl.BlockSpec((tn, SB_WIDTH), lambda s, pt: (s, 0)),
        scratch_shapes=[
            pltpu.VMEM((2, ppc) + page_shape, F32), pltpu.VMEM((2, ppc) + page_shape, F32),
            pltpu.SemaphoreType.DMA((2, 2)),
            pltpu.VMEM((rows, LANES), BF16), pltpu.VMEM((rows, LANES), F32), pltpu.VMEM((rows, LANES), F32),
        ],
    )
    return pl.pallas_call(
        functools.partial(_ssb_kernel, layer=layer, npages=npages, ppc=ppc, tn=tn), grid_spec=grid_spec,
        out_shape=jax.ShapeDtypeStruct((nseq * tn, SB_WIDTH), F32),
        compiler_params=_cparams(("arbitrary",)), name="ssb",
    )(page_table, qsb, kn, vn, tri, cache_kt, cache_vt)


def _smla_kernel(pt_ref, ql_ref, qr_ref, cn_ref, krn_ref, wuv_ref, *rest, npg, tn):
    cs = rest[0:npg]
    rts = rest[npg:2 * npg]
    omla_ref = rest[2 * npg]
    ql_s, qr_s, m_ref, l_ref, acc_m = rest[2 * npg + 1:]
    j = pl.program_id(1)
    rows = MLA_HEADS * tn

    @pl.when(j == 0)
    def _():
        ql_s[...] = ql_ref[...].astype(F32).reshape(rows, MLA_KV_RANK).astype(BF16)
        qr_s[...] = qr_ref[...].astype(F32).reshape(rows, MLA_ROPE_DIM).astype(BF16)
        cnb = _pad_new(cn_ref, tn)
        tpos, spos = _new_positions(rows, tn)
        s = _dot_nt(ql_s[...], cnb) + _dot_nt(qr_s[...], _pad_new(krn_ref, tn))
        s = jnp.where(spos <= tpos, s, NEG_BIG)
        m_ref[...] = jnp.full_like(m_ref, NEG_BIG)
        l_ref[...] = jnp.zeros_like(l_ref)
        acc_m[...] = jnp.zeros_like(acc_m)
        _mla_update(s, cnb, m_ref, l_ref, acc_m)

    page_slice = _page_slice
    ql = ql_s[...]
    qr = qr_s[...]
    s = jnp.concatenate([_dot_nt(ql, cs[i][0, 0].astype(BF16)) + _dot(qr, rts[i][0, 0].astype(BF16))
                         for i in range(npg)], axis=1)
    m_prev = m_ref[...]
    m_new = jnp.maximum(m_prev, jnp.max(s, axis=1, keepdims=True))
    p = jnp.exp2((s - _rep_lanes(m_new, npg * PAGE_SIZE)) * MLA_EXP2_SCALE)
    alpha = jnp.exp2((m_prev - m_new) * MLA_EXP2_SCALE)
    l_ref[...] = alpha * l_ref[...] + jnp.sum(p, axis=1, keepdims=True)
    pb = p.astype(BF16)
    pv = None
    for i in range(npg):
        t = _dot(page_slice(pb, i), cs[i][0, 0].astype(BF16))
        pv = t if pv is None else pv + t
    acc_m[...] = _rep_lanes(alpha, MLA_KV_RANK) * acc_m[...] + pv
    m_ref[...] = m_new

    @pl.when(j == pl.num_programs(1) - 1)
    def _():
        _mla_finish(omla_ref, wuv_ref, l_ref, acc_m, tn)


def _smla_call(layer, page_table, qlat, qrope, cn, krn, wuv, cache_c, cache_rt, tn):
    nseq, npages = page_table.shape
    npg = min(16, npages)
    nsteps = npages // npg
    rows = MLA_HEADS * tn

    def seq3(s, j, pt):
        return (0, s, 0)

    def seq2(s, j, pt):
        return (s, 0)

    def page_map(i):
        return lambda s, j, pt: (layer, pt[s, npg * j + i], 0, 0)

    in_specs = [
        pl.BlockSpec((MLA_HEADS, tn, MLA_KV_RANK), seq3),
        pl.BlockSpec((MLA_HEADS, tn, MLA_ROPE_DIM), seq3),
        pl.BlockSpec((tn, MLA_KV_RANK), seq2),
        pl.BlockSpec((tn, MLA_ROPE_DIM), seq2),
        pl.BlockSpec(wuv.shape, lambda s, j, pt: (0, 0, 0)),
    ]
    operands = [qlat, qrope, cn, krn, wuv]
    for cache in (cache_c, cache_rt):
        for i in range(npg):
            in_specs.append(pl.BlockSpec((1, 1) + cache.shape[2:], page_map(i)))
            operands.append(cache)
    grid_spec = pltpu.PrefetchScalarGridSpec(
        num_scalar_prefetch=1, grid=(nseq, nsteps), in_specs=in_specs,
        out_specs=pl.BlockSpec((tn, MLA_WIDTH), seq2),
        scratch_shapes=[
            pltpu.VMEM((rows, MLA_KV_RANK), BF16), pltpu.VMEM((rows, MLA_ROPE_DIM), BF16),
            pltpu.VMEM((rows, LANES), F32), pltpu.VMEM((rows, LANES), F32),
            pltpu.VMEM((rows, MLA_KV_RANK), F32),
        ],
    )
    return pl.pallas_call(
        functools.partial(_smla_kernel, npg=npg, tn=tn), grid_spec=grid_spec,
        out_shape=jax.ShapeDtypeStruct((nseq * tn, MLA_WIDTH), F32),
        compiler_params=_cparams(("parallel", "arbitrary")), name="smla",
    )(page_table, *operands)


def _out_kernel(x_ref, brc_ref, osb_ref, omla_ref, ng_ref, wz_ref, wg_ref, woc_ref, wos_ref, wom_ref, wo_ref,
                fg_ref, o_ref, *, final):
    x = x_ref[...]
    ub = _rms(x, ng_ref[...]).astype(BF16)
    z = _dot(ub, wz_ref[...])
    merged = None
    branches = ((brc_ref, woc_ref, CONV_CH), (osb_ref, wos_ref, SB_WIDTH), (omla_ref, wom_ref, MLA_WIDTH))
    off = 0
    for i, (b_ref, w_ref, width) in enumerate(branches):
        br = (b_ref[...] * _silu(z[:, off:off + width])).astype(BF16)
        off += width
        gate = _sigmoid(_dot(ub, wg_ref[:, i * D_MODEL:(i + 1) * D_MODEL]))
        term = gate * _dot(br, w_ref[...])
        merged = term if merged is None else merged + term
    out = x + _dot(merged.astype(BF16), wo_ref[...])
    if final:
        out = _rms(out, fg_ref[...])
    o_ref[...] = out


def _out_call(x, brc, osb, omla, lw, final_g, final, tm):
    n = x.shape[0]
    row = lambda i: (i, 0)
    c2 = lambda i: (0, 0)
    wnames = ('wz', 'wg', 'woc', 'wos', 'wom', 'wo')
    in_specs = [pl.BlockSpec((tm, D_MODEL), row), pl.BlockSpec((tm, CONV_CH), row),
                pl.BlockSpec((tm, SB_WIDTH), row), pl.BlockSpec((tm, MLA_WIDTH), row),
                pl.BlockSpec((1, D_MODEL), c2)]
    in_specs += [pl.BlockSpec(lw[k].shape, c2) for k in wnames]
    in_specs += [pl.BlockSpec((1, D_MODEL), c2)]
    return pl.pallas_call(
        functools.partial(_out_kernel, final=final), grid=(n // tm,), in_specs=in_specs,
        out_specs=pl.BlockSpec((tm, D_MODEL), row), out_shape=jax.ShapeDtypeStruct((n, D_MODEL), F32),
        compiler_params=_cparams(("parallel",)), name="out",
    )(x, brc, osb, omla, lw['norm_g'], *[lw[k] for k in wnames], final_g)


def _layer_weights(l, norm_g, w_in, conv_w, conv_b, conv_ln_g, conv_ln_b, mla_q_norm_g, mla_w_uq, mla_kv_norm_g,
                   mla_w_uk, mla_w_uv, w_out_conv, w_out_sb, w_out_mla, w_o):
    wi = w_in[l]
    col = lambda i: wi[:, IN_OFF[i]:IN_OFF[i + 1]]
    half = MLA_ROPE_DIM // 2
    wq = col(3).reshape(D_MODEL, SB_HEADS, SB_HEAD_DIM)
    zq = jnp.zeros_like(wq)
    grp = (jnp.arange(SB_HEADS) // SB_GROUP)[None, :, None]
    wq_pad = jnp.where(grp == 0, jnp.concatenate([wq, zq], -1), jnp.concatenate([zq, wq], -1))
    wsb = jnp.concatenate([wq_pad.reshape(D_MODEL, SB_HEADS * LANES), col(4), col(5)], axis=1)
    wkr = col(9)
    wkr_sw = jnp.concatenate([-wkr[:, half:], wkr[:, :half]], axis=1)
    padk = jnp.zeros((D_MODEL, LANES - MLA_ROPE_DIM), F32)
    wmla = jnp.concatenate([col(7), col(8), wkr, padk, wkr_sw, padk], axis=1)
    wuq = mla_w_uq[l].reshape(MLA_Q_RANK, MLA_HEADS, MLA_NOPE_DIM + MLA_ROPE_DIM)
    wuq_n = wuq[:, :, :MLA_NOPE_DIM].reshape(MLA_Q_RANK, MLA_HEADS * MLA_NOPE_DIM)
    wuq_r = wuq[:, :, MLA_NOPE_DIM:]
    wuq_r1 = wuq_r.reshape(MLA_Q_RANK, MLA_HEADS * MLA_ROPE_DIM)
    wuq_r2 = jnp.concatenate([-wuq_r[:, :, half:], wuq_r[:, :, :half]], axis=-1).reshape(MLA_Q_RANK, -1)
    wuq_cat = jnp.concatenate([wuq_n, wuq_r1, wuq_r2], axis=1)
    ukt = jnp.transpose(mla_w_uk[l], (1, 2, 0))
    zk = jnp.zeros_like(ukt)
    odd = (jnp.arange(MLA_HEADS) % 2)[:, None, None]
    wuk = jnp.where(odd == 0, jnp.concatenate([ukt, zk], 1), jnp.concatenate([zk, ukt], 1))
    wuv2 = mla_w_uv[l].reshape(MLA_KV_RANK, MLA_WIDTH)
    hcol = (jnp.arange(MLA_WIDTH) // MLA_V_DIM)[None, None, :]
    wuv = jnp.where(hcol == jnp.arange(MLA_HEADS)[:, None, None], wuv2[None], 0.0)
    order = np.array(SB_HEAD_ORDER)
    wz_sb = col(6).reshape(D_MODEL, SB_HEADS, SB_HEAD_DIM)[:, order].reshape(D_MODEL, SB_WIDTH)
    wos = w_out_sb[l].reshape(SB_HEADS, SB_HEAD_DIM, D_MODEL)[order].reshape(SB_WIDTH, D_MODEL)
    bf = lambda a: a.astype(BF16)
    return {
        'norm_g': norm_g[l][None], 'wa': bf(jnp.concatenate([col(0), col(1)], axis=1)), 'wsb': bf(wsb),
        'wmla': bf(wmla), 'qg': mla_q_norm_g[l][None], 'wuq': bf(wuq_cat), 'kvg': mla_kv_norm_g[l][None],
        'wuk': bf(wuk), 'wuv': bf(wuv),
        'conv_w': conv_w[l], 'conv_b': conv_b[l][None], 'conv_ln_g': conv_ln_g[l][None],
        'conv_ln_b': conv_ln_b[l][None],
        'wz': bf(jnp.concatenate([col(2), wz_sb, col(10)], axis=1)), 'wg': bf(col(11)),
        'woc': bf(w_out_conv[l]), 'wos': bf(wos), 'wom': bf(w_out_mla[l]), 'wo': bf(w_o[l]),
    }


def _rope_tables(pos):
    inv = ROPE_THETA ** (-jnp.arange(0, MLA_ROPE_DIM, 2, dtype=F32) / MLA_ROPE_DIM)
    ang = pos.astype(F32)[:, None] * inv[None, :]
    cos = jnp.cos(ang)
    sin = jnp.sin(ang)
    return (jnp.tile(jnp.concatenate([cos, cos], axis=1), (1, MLA_HEADS)),
            jnp.tile(jnp.concatenate([sin, sin], axis=1), (1, MLA_HEADS)))


def kernel(x_prompt, x_sample, cache_sb_k, cache_sb_v, cache_mla_latent, cache_mla_rope, state_conv, page_table,
           norm_g, w_in, conv_w, conv_b, conv_ln_g, conv_ln_b, mla_q_norm_g, mla_w_uq, mla_kv_norm_g, mla_w_uk,
           mla_w_uv, w_out_conv, w_out_sb, w_out_mla, w_o, final_norm_g):
    b_p, seq, _ = x_prompt.shape
    b_s, t_new, _ = x_sample.shape
    depth = norm_g.shape[0]
    n_pool = cache_sb_k.shape[1]
    past_len = page_table.shape[1] * PAGE_SIZE
    n_p = b_p * seq
    n_s = b_s * t_new
    tm_p = min(256, n_p)
    tm_s = min(256, n_s)

    cos_p, sin_p = _rope_tables(jnp.arange(seq, dtype=jnp.int32))
    pos_s = past_len + (jnp.arange(tm_s, dtype=jnp.int32) % t_new)
    cos_s, sin_s = _rope_tables(pos_s)
    cache_kt = jnp.transpose(cache_sb_k, (0, 1, 3, 4, 2)).reshape(depth, n_pool, SB_KV_WIDTH, PAGE_SIZE)
    cache_vt = jnp.transpose(cache_sb_v, (0, 1, 3, 4, 2)).reshape(depth, n_pool, SB_KV_WIDTH, PAGE_SIZE)
    cache_rt = jnp.transpose(cache_mla_rope, (0, 1, 3, 2))
    fg = final_norm_g[None]
    zero_prefix = jnp.zeros((b_p, CONV_HALO, CONV_CH), F32)
    halo_pad = CONV_HALO - (CONV_WIDTH - 1)

    xp = x_prompt.reshape(n_p, D_MODEL)
    xs = x_sample.reshape(n_s, D_MODEL)
    st_p, st_s = [], []
    for l in range(depth):
        lw = _layer_weights(l, norm_g, w_in, conv_w, conv_b, conv_ln_g, conv_ln_b, mla_q_norm_g, mla_w_uq,
                            mla_kv_norm_g, mla_w_uk, mla_w_uv, w_out_conv, w_out_sb, w_out_mla, w_o)
        final = l == depth - 1
        h, qsb, k, v, qlat, qrope, ckv, kr, kb16, vb16, ckvb16, krb16 = _pre_call(xp, cos_p, sin_p, lw, tm_p)
        h3 = h.reshape(b_p, seq, CONV_CH)
        brc = _conv_call(h3, zero_prefix, lw).reshape(n_p, CONV_CH)
        osb = _sbp_call(qsb, kb16, vb16, b_p, seq)
        omla = _mlap_call(qlat, qrope, ckvb16, krb16, lw['wuv'], b_p, seq)
        xp = _out_call(xp, brc, osb, omla, lw, fg, final, tm_p)
        hist_p = jnp.concatenate([jnp.zeros((b_p, CONV_WIDTH - 1, CONV_CH), F32), h3], axis=1)
        st_p.append((k.reshape(b_p, seq, SB_KV_HEADS, SB_HEAD_DIM), v.reshape(b_p, seq, SB_KV_HEADS, SB_HEAD_DIM),
                     ckv.reshape(b_p, seq, MLA_KV_RANK), kr.reshape(b_p, seq, MLA_ROPE_DIM),
                     hist_p[:, -(CONV_WIDTH - 1):]))
        h, qsb, k, v, qlat, qrope, ckv, kr = _pre_call(xs, cos_s, sin_s, lw, tm_s)[:8]
        h3 = h.reshape(b_s, t_new, CONV_CH)
        prefix = jnp.pad(state_conv[l], ((0, 0), (halo_pad, 0), (0, 0)))
        brc = _conv_call(h3, prefix, lw).reshape(n_s, CONV_CH)
        osb = _ssb_call(l, page_table, qsb, k, v, cache_kt, cache_vt, t_new)
        omla = _smla_call(l, page_table, qlat, qrope, ckv, kr, lw['wuv'], cache_mla_latent, cache_rt, t_new)
        xs = _out_call(xs, brc, osb, omla, lw, fg, final, tm_s)
        hist_s = jnp.concatenate([state_conv[l], h3], axis=1)
        st_s.append((k.reshape(b_s, t_new, SB_KV_HEADS, SB_HEAD_DIM), v.reshape(b_s, t_new, SB_KV_HEADS, SB_HEAD_DIM),
                     ckv.reshape(b_s, t_new, MLA_KV_RANK), kr.reshape(b_s, t_new, MLA_ROPE_DIM),
                     hist_s[:, -(CONV_WIDTH - 1):]))

    outs = [xp.reshape(b_p, seq, D_MODEL), xs.reshape(b_s, t_new, D_MODEL)]
    for st in (st_p, st_s):
        for i in range(5):
            outs.append(jnp.stack([s[i] for s in st]))
    return tuple(outs)
```

```python
import functools

import numpy as np
import jax
import jax.numpy as jnp
from jax import lax
from jax.experimental import pallas as pl
from jax.experimental.pallas import tpu as pltpu

F32 = jnp.float32
BF16 = jnp.bfloat16

D_MODEL = 1024
PAGE_SIZE = 128
CONV_CH = 512
CONV_WIDTH = 31
SB_HEADS = 8
SB_KV_HEADS = 2
SB_GROUP = SB_HEADS // SB_KV_HEADS
SB_HEAD_DIM = 64
SB_WIDTH = SB_HEADS * SB_HEAD_DIM
SB_KV_WIDTH = SB_KV_HEADS * SB_HEAD_DIM
MLA_HEADS = 8
MLA_Q_RANK = 256
MLA_KV_RANK = 256
MLA_NOPE_DIM = 64
MLA_ROPE_DIM = 32
MLA_V_DIM = 64
MLA_WIDTH = MLA_HEADS * MLA_V_DIM
MLA_SCALE = (MLA_NOPE_DIM + MLA_ROPE_DIM) ** -0.5
ROPE_THETA = 10000.0
N_BRANCHES = 3
NORM_EPS = 1e-6
IN_SIZES = (CONV_CH, CONV_CH, CONV_CH, SB_WIDTH, SB_KV_WIDTH, SB_KV_WIDTH, SB_WIDTH, MLA_Q_RANK, MLA_KV_RANK,
            MLA_ROPE_DIM, MLA_WIDTH, N_BRANCHES * D_MODEL)
IN_OFF = tuple(int(v) for v in np.concatenate([[0], np.cumsum(IN_SIZES)]))

LANES = 128
CONV_HALO = 32
LOG2E = 1.4426950408889634
MLA_EXP2_SCALE = MLA_SCALE * LOG2E
NEG_BIG = -1e30
SB_DEAD_LOG2 = 150.0
VMEM_LIMIT = 48 * 1024 * 1024
SB_HEAD_ORDER = tuple(h for p in range(SB_GROUP) for h in (p, SB_GROUP + p))


def _dot(a, b):
    return jnp.dot(a, b, preferred_element_type=F32)


def _dot_nt(a, b):
    return lax.dot_general(a, b, (((1,), (1,)), ((), ())), preferred_element_type=F32)


def _sigmoid(x):
    return 1.0 / (1.0 + jnp.exp(-x))


def _silu(x):
    return x * _sigmoid(x)


def _rms(x, g):
    return x * lax.rsqrt(jnp.mean(x * x, axis=-1, keepdims=True) + NORM_EPS) * g


def _rep_lanes(x, width):
    reps = width // LANES
    return x if reps == 1 else jnp.concatenate([x] * reps, axis=1)


def _cparams(sem):
    return pltpu.CompilerParams(dimension_semantics=sem, vmem_limit_bytes=VMEM_LIMIT)


def _pre_kernel(x_ref, cos_ref, sin_ref, ng_ref, wa_ref, wsb_ref, wmla_ref, qg_ref, wuq_ref, kvg_ref, wuk_ref,
                h_ref, qsb_ref, k_ref, v_ref, qlat_ref, qrope_ref, ckv_ref, kr_ref, kb_ref, vb_ref, ckvb_ref, krb_ref):
    ub = _rms(x_ref[...], ng_ref[...]).astype(BF16)
    ag = _dot(ub, wa_ref[...])
    h_ref[...] = ag[:, :CONV_CH] * _sigmoid(ag[:, CONV_CH:])
    sb = _dot(ub, wsb_ref[...])
    for hd in range(SB_HEADS):
        qsb_ref[hd] = (sb[:, LANES * hd:LANES * (hd + 1)] * (SB_HEAD_DIM ** -0.5)).astype(BF16)
    k = sb[:, SB_HEADS * LANES:SB_HEADS * LANES + SB_KV_WIDTH]
    v = sb[:, SB_HEADS * LANES + SB_KV_WIDTH:]
    k_ref[...] = k
    v_ref[...] = v
    kb_ref[...] = k.astype(BF16)
    vb_ref[...] = v.astype(BF16)
    m = _dot(ub, wmla_ref[...])
    cos = cos_ref[...]
    sin = sin_ref[...]
    ckv = _rms(m[:, MLA_Q_RANK:MLA_Q_RANK + MLA_KV_RANK], kvg_ref[...])
    o1 = MLA_Q_RANK + MLA_KV_RANK
    kr = (m[:, o1:o1 + MLA_ROPE_DIM] * cos[:, :MLA_ROPE_DIM]
          + m[:, o1 + LANES:o1 + LANES + MLA_ROPE_DIM] * sin[:, :MLA_ROPE_DIM])
    ckv_ref[...] = ckv
    kr_ref[...] = kr
    ckvb_ref[...] = ckv.astype(BF16)
    krb_ref[...] = kr.astype(BF16)
    cqn = _rms(m[:, :MLA_Q_RANK], qg_ref[...]).astype(BF16)
    qq = _dot(cqn, wuq_ref[...])
    nw = MLA_HEADS * MLA_NOPE_DIM
    rw = MLA_HEADS * MLA_ROPE_DIM
    rot = qq[:, nw:nw + rw] * cos + qq[:, nw + rw:] * sin
    for hd in range(MLA_HEADS):
        qrope_ref[hd] = rot[:, MLA_ROPE_DIM * hd:MLA_ROPE_DIM * (hd + 1)].astype(BF16)
        pair = hd // 2
        qn = qq[:, LANES * pair:LANES * (pair + 1)].astype(BF16)
        qlat_ref[hd] = _dot(qn, wuk_ref[hd]).astype(BF16)


def _pre_call(x, cos, sin, lw, tm):
    n = x.shape[0]
    nblk = cos.shape[0] // tm
    row = lambda i: (i, 0)
    hrow = lambda i: (0, i, 0)
    c2 = lambda i: (0, 0)
    c3 = lambda i: (0, 0, 0)
    tab = lambda i: (i % nblk, 0)
    rw = MLA_HEADS * MLA_ROPE_DIM
    in_specs = [
        pl.BlockSpec((tm, D_MODEL), row),
        pl.BlockSpec((tm, rw), tab),
        pl.BlockSpec((tm, rw), tab),
        pl.BlockSpec((1, D_MODEL), c2),
        pl.BlockSpec(lw['wa'].shape, c2),
        pl.BlockSpec(lw['wsb'].shape, c2),
        pl.BlockSpec(lw['wmla'].shape, c2),
        pl.BlockSpec((1, MLA_Q_RANK), c2),
        pl.BlockSpec(lw['wuq'].shape, c2),
        pl.BlockSpec((1, MLA_KV_RANK), c2),
        pl.BlockSpec(lw['wuk'].shape, c3),
    ]
    out_shape = [
        jax.ShapeDtypeStruct((n, CONV_CH), F32),
        jax.ShapeDtypeStruct((SB_HEADS, n, LANES), BF16),
        jax.ShapeDtypeStruct((n, SB_KV_WIDTH), F32),
        jax.ShapeDtypeStruct((n, SB_KV_WIDTH), F32),
        jax.ShapeDtypeStruct((MLA_HEADS, n, MLA_KV_RANK), BF16),
        jax.ShapeDtypeStruct((MLA_HEADS, n, MLA_ROPE_DIM), BF16),
        jax.ShapeDtypeStruct((n, MLA_KV_RANK), F32),
        jax.ShapeDtypeStruct((n, MLA_ROPE_DIM), F32),
        jax.ShapeDtypeStruct((n, SB_KV_WIDTH), BF16),
        jax.ShapeDtypeStruct((n, SB_KV_WIDTH), BF16),
        jax.ShapeDtypeStruct((n, MLA_KV_RANK), BF16),
        jax.ShapeDtypeStruct((n, MLA_ROPE_DIM), BF16),
    ]
    out_specs = [
        pl.BlockSpec((tm, CONV_CH), row),
        pl.BlockSpec((SB_HEADS, tm, LANES), hrow),
        pl.BlockSpec((tm, SB_KV_WIDTH), row),
        pl.BlockSpec((tm, SB_KV_WIDTH), row),
        pl.BlockSpec((MLA_HEADS, tm, MLA_KV_RANK), hrow),
        pl.BlockSpec((MLA_HEADS, tm, MLA_ROPE_DIM), hrow),
        pl.BlockSpec((tm, MLA_KV_RANK), row),
        pl.BlockSpec((tm, MLA_ROPE_DIM), row),
        pl.BlockSpec((tm, SB_KV_WIDTH), row),
        pl.BlockSpec((tm, SB_KV_WIDTH), row),
        pl.BlockSpec((tm, MLA_KV_RANK), row),
        pl.BlockSpec((tm, MLA_ROPE_DIM), row),
    ]
    return pl.pallas_call(
        _pre_kernel, grid=(n // tm,), in_specs=in_specs, out_specs=out_specs, out_shape=out_shape,
        compiler_params=_cparams(("parallel",)), name="pre",
    )(x, cos, sin, lw['norm_g'], lw['wa'], lw['wsb'], lw['wmla'], lw['qg'], lw['wuq'], lw['kvg'], lw['wuk'])


def _conv_kernel(h_ref, pre_ref, w_ref, b_ref, g_ref, beta_ref, o_ref, win_ref, y_ref, *, tt, rc):
    t = pl.program_id(1)

    @pl.when(t == 0)
    def _():
        win_ref[0:CONV_HALO, :] = pre_ref[0]

    win_ref[CONV_HALO:CONV_HALO + tt, :] = h_ref[0]
    shift = CONV_HALO - (CONV_WIDTH - 1)
    for r in range(tt // rc):
        for c in range(CONV_CH // LANES):
            cs = slice(c * LANES, (c + 1) * LANES)
            acc = jnp.broadcast_to(b_ref[:, cs], (rc, LANES))
            for k in range(CONV_WIDTH):
                lo = r * rc + shift + k
                acc = acc + w_ref[k:k + 1, cs] * win_ref[lo:lo + rc, cs]
            y_ref[r * rc:(r + 1) * rc, cs] = acc
    y = y_ref[...]
    mu = jnp.mean(y, axis=-1, keepdims=True)
    yc = y - mu
    ln = yc * lax.rsqrt(jnp.mean(yc * yc, axis=-1, keepdims=True) + NORM_EPS) * g_ref[...] + beta_ref[...]
    o_ref[0] = _silu(ln)
    win_ref[0:CONV_HALO, :] = win_ref[tt:tt + CONV_HALO, :]


def _conv_call(h, prefix, lw):
    b, t, _ = h.shape
    tt = min(256, t)
    rc = min(32, tt)
    c2 = lambda i, j: (0, 0)
    kern = functools.partial(_conv_kernel, tt=tt, rc=rc)
    return pl.pallas_call(
        kern, grid=(b, t // tt),
        in_specs=[
            pl.BlockSpec((1, tt, CONV_CH), lambda i, j: (i, j, 0)),
            pl.BlockSpec((1, CONV_HALO, CONV_CH), lambda i, j: (i, 0, 0)),
            pl.BlockSpec((CONV_WIDTH, CONV_CH), c2),
            pl.BlockSpec((1, CONV_CH), c2),
            pl.BlockSpec((1, CONV_CH), c2),
            pl.BlockSpec((1, CONV_CH), c2),
        ],
        out_specs=pl.BlockSpec((1, tt, CONV_CH), lambda i, j: (i, j, 0)),
        out_shape=jax.ShapeDtypeStruct((b, t, CONV_CH), F32),
        scratch_shapes=[pltpu.VMEM((tt + CONV_HALO, CONV_CH), F32), pltpu.VMEM((tt, CONV_CH), F32)],
        compiler_params=_cparams(("parallel", "arbitrary")), name="conv",
    )(h, prefix, lw['conv_w'], lw['conv_b'], lw['conv_ln_g'], lw['conv_ln_b'])


def _sb_logits(z):
    z2 = z * LOG2E
    return z2, jnp.maximum(z2, 0.0) + jnp.log2(1.0 + jnp.exp2(-jnp.abs(z2)))


def _split_bf16(x):
    hi = x.astype(BF16)
    return hi, (x - hi.astype(F32)).astype(BF16)


def _sb_tile(z, tri, run, mask):
    z2, sp = _sb_logits(z)
    if mask is not None:
        sp = jnp.where(mask, sp, 0.0)
    hi, lo = _split_bf16(sp)
    cum = _dot(hi, tri) + _dot(lo, tri)
    w = jnp.exp2(z2 - sp - cum - run)
    if mask is not None:
        w = jnp.where(mask, w, 0.0)
    return w, jnp.sum(sp, axis=1, keepdims=True)


def _tri_matrix(tk, ones_cols=0):
    j = np.arange(tk)[:, None]
    s = np.arange(tk + ones_cols)[None, :]
    return jnp.asarray(((j > s) | (s >= tk)).astype(np.float32), dtype=BF16)


def _sb_pair_store(o_ref, acc0, acc1, rows):
    lane = lax.broadcasted_iota(jnp.int32, (rows, LANES), 1)
    for p in range(SB_GROUP):
        o_ref[:, LANES * p:LANES * (p + 1)] = jnp.where(lane < SB_HEAD_DIM, acc0[p * rows:(p + 1) * rows],
                                                         acc1[p * rows:(p + 1) * rows])


def _sbp_kernel(q_ref, k_ref, v_ref, tri_ref, o_ref, acc_ref, run_ref, *, tq):
    qi = pl.program_id(1)
    rows = SB_GROUP * tq
    acc_ref[...] = jnp.zeros_like(acc_ref)
    run_ref[...] = jnp.zeros_like(run_ref)

    def block(kblk, masked):
        off = pl.multiple_of(kblk * tq, tq)
        kb = k_ref[pl.ds(off, tq), :]
        vb = v_ref[pl.ds(off, tq), :]
        tri = tri_ref[...]
        mask = None
        if masked:
            tpos = lax.broadcasted_iota(jnp.int32, (SB_GROUP, tq, tq), 1).reshape(rows, tq)
            spos = lax.broadcasted_iota(jnp.int32, (rows, tq), 1)
            mask = spos < tpos
        run_min = None
        for g in range(SB_KV_HEADS):
            q = q_ref[SB_GROUP * g:SB_GROUP * (g + 1)].reshape(rows, LANES)
            z = _dot_nt(q, kb)
            run = run_ref[g]
            w, tot = _sb_tile(z, tri, _rep_lanes(run, tq), mask)
            acc_ref[g] += _dot(w.astype(BF16), vb)
            run = run + tot
            run_ref[g] = run
            run_min = run if run_min is None else jnp.minimum(run_min, run)
        return (jnp.min(run_min) < SB_DEAD_LOG2).astype(jnp.int32)

    def more(carry):
        kblk, live = carry
        return jnp.logical_and(kblk >= 0, live == 1)

    def step(carry):
        kblk, _ = carry
        return kblk - 1, block(kblk, False)

    lax.while_loop(more, step, (qi - 1, block(qi, True)))
    _sb_pair_store(o_ref, acc_ref[0], acc_ref[1], tq)


def _sbp_call(qsb, kb16, vb16, b, t):
    tq = min(256, t)
    nq = t // tq
    n = b * t
    return pl.pallas_call(
        functools.partial(_sbp_kernel, tq=tq), grid=(b, nq),
        in_specs=[
            pl.BlockSpec((SB_HEADS, tq, LANES), lambda bi, qi: (0, bi * nq + qi, 0)),
            pl.BlockSpec((t, SB_KV_WIDTH), lambda bi, qi: (bi, 0)),
            pl.BlockSpec((t, SB_KV_WIDTH), lambda bi, qi: (bi, 0)),
            pl.BlockSpec((tq, tq), lambda bi, qi: (0, 0)),
        ],
        out_specs=pl.BlockSpec((tq, SB_WIDTH), lambda bi, qi: (bi * nq + qi, 0)),
        out_shape=jax.ShapeDtypeStruct((n, SB_WIDTH), F32),
        scratch_shapes=[pltpu.VMEM((SB_KV_HEADS, SB_GROUP * tq, LANES), F32),
                        pltpu.VMEM((SB_KV_HEADS, SB_GROUP * tq, LANES), F32)],
        compiler_params=_cparams(("parallel", "arbitrary")), name="sbp",
    )(qsb, kb16, vb16, _tri_matrix(tq))


def _mla_update(s, cb, m_ref, l_ref, acc_ref):
    tk = s.shape[1]
    m_prev = m_ref[...]
    m_new = jnp.maximum(m_prev, jnp.max(s, axis=1, keepdims=True))
    p = jnp.exp2((s - _rep_lanes(m_new, tk)) * MLA_EXP2_SCALE)
    alpha = jnp.exp2((m_prev - m_new) * MLA_EXP2_SCALE)
    l_ref[...] = alpha * l_ref[...] + jnp.sum(p, axis=1, keepdims=True)
    acc_ref[...] = _rep_lanes(alpha, acc_ref.shape[1]) * acc_ref[...] + _dot(p.astype(BF16), cb)
    m_ref[...] = m_new


def _mla_finish(o_ref, wuv_ref, l_ref, acc_ref, rows):
    inv = _rep_lanes(1.0 / l_ref[...], acc_ref.shape[1])
    o = None
    for hd in range(MLA_HEADS):
        rs = slice(hd * rows, (hd + 1) * rows)
        oh = _dot((acc_ref[rs] * inv[rs]).astype(BF16), wuv_ref[hd])
        o = oh if o is None else o + oh
    o_ref[...] = o


def _mlap_kernel(ql_ref, qr_ref, c_ref, kr_ref, wuv_ref, o_ref, s_ref, m_ref, l_ref, acc_ref, *, tq, tk):
    qi = pl.program_id(1)
    rows = MLA_HEADS * tq
    ql = ql_ref[...].reshape(rows, MLA_KV_RANK)
    qr = qr_ref[...].reshape(rows, MLA_ROPE_DIM)
    m_ref[...] = jnp.full_like(m_ref, NEG_BIG)
    l_ref[...] = jnp.zeros_like(l_ref)
    acc_ref[...] = jnp.zeros_like(acc_ref)

    def rows_of(kblk):
        return pl.ds(pl.multiple_of(kblk * tk, tk), tk)

    def scores(kblk):
        return _dot_nt(ql, c_ref[rows_of(kblk), :]) + _dot_nt(qr, kr_ref[rows_of(kblk), :])

    def update(slot, kblk, masked):
        s = s_ref[slot]
        if masked:
            tpos = qi * tq + lax.broadcasted_iota(jnp.int32, (MLA_HEADS, tq, tk), 1).reshape(rows, tk)
            spos = kblk * tk + lax.broadcasted_iota(jnp.int32, (rows, tk), 1)
            s = jnp.where(spos <= tpos, s, NEG_BIG)
        _mla_update(s, c_ref[rows_of(kblk), :], m_ref, l_ref, acc_ref)

    n_full = (qi * tq + 1) // tk
    s_ref[0] = scores(0)

    def pair(jj, carry):
        j = 2 * jj
        s_ref[1] = scores(j + 1)
        update(0, j, False)
        s_ref[0] = scores(j + 2)
        update(1, j + 1, False)
        return carry

    lax.fori_loop(0, n_full // 2, pair, 0)
    odd = n_full % 2 == 1

    @pl.when(odd)
    def _():
        s_ref[1] = scores(n_full)
        update(0, n_full - 1, False)
        update(1, n_full, True)

    @pl.when(jnp.logical_not(odd))
    def _():
        update(0, n_full, True)

    _mla_finish(o_ref, wuv_ref, l_ref, acc_ref, tq)


def _mlap_call(qlat, qrope, ckvb16, krb16, wuv, b, t):
    tq = min(128, t)
    tk = min(256, t)
    assert tk % tq == 0
    nq = t // tq
    n = b * t
    rows = MLA_HEADS * tq
    return pl.pallas_call(
        functools.partial(_mlap_kernel, tq=tq, tk=tk), grid=(b, nq),
        in_specs=[
            pl.BlockSpec((MLA_HEADS, tq, MLA_KV_RANK), lambda bi, qi: (0, bi * nq + qi, 0)),
            pl.BlockSpec((MLA_HEADS, tq, MLA_ROPE_DIM), lambda bi, qi: (0, bi * nq + qi, 0)),
            pl.BlockSpec((t, MLA_KV_RANK), lambda bi, qi: (bi, 0)),
            pl.BlockSpec((t, MLA_ROPE_DIM), lambda bi, qi: (bi, 0)),
            pl.BlockSpec(wuv.shape, lambda bi, qi: (0, 0, 0)),
        ],
        out_specs=pl.BlockSpec((tq, MLA_WIDTH), lambda bi, qi: (bi * nq + qi, 0)),
        out_shape=jax.ShapeDtypeStruct((n, MLA_WIDTH), F32),
        scratch_shapes=[pltpu.VMEM((2, rows, tk), F32), pltpu.VMEM((rows, LANES), F32),
                        pltpu.VMEM((rows, LANES), F32), pltpu.VMEM((rows, MLA_KV_RANK), F32)],
        compiler_params=_cparams(("parallel", "arbitrary")), name="mlap",
    )(qlat, qrope, ckvb16, krb16, wuv)


def _pad_new(ref, tn):
    x = ref[...]
    return jnp.concatenate([x, jnp.zeros((PAGE_SIZE - tn, x.shape[1]), F32)], axis=0).astype(BF16)


def _new_positions(rows, tn):
    tpos = lax.broadcasted_iota(jnp.int32, (rows // tn, tn, PAGE_SIZE), 1).reshape(rows, PAGE_SIZE)
    spos = lax.broadcasted_iota(jnp.int32, (rows, PAGE_SIZE), 1)
    return tpos, spos


def _page_slice(x, i):
    return x[:, i * PAGE_SIZE:(i + 1) * PAGE_SIZE]


def _ssb_kernel(pt_ref, q_ref, kn_ref, vn_ref, tri_ref, kt_hbm, vt_hbm, o_ref, kbuf, vbuf, sem, qs, acc_ref, run_ref,
                *, layer, npages, ppc, tn):
    s = pl.program_id(0)
    slot = s % 2
    rows = SB_HEADS * tn
    tc = ppc * PAGE_SIZE
    nchunks = npages // ppc

    def chunk_copies(seq, chunk, slot_):
        cps = []
        for i in range(ppc):
            page = pt_ref[seq, npages - ppc * (chunk + 1) + i]
            cps.append(pltpu.make_async_copy(kt_hbm.at[layer, page], kbuf.at[slot_, i], sem.at[slot_, 0]))
            cps.append(pltpu.make_async_copy(vt_hbm.at[layer, page], vbuf.at[slot_, i], sem.at[slot_, 1]))
        return cps

    @pl.when(s == 0)
    def _():
        for cp in chunk_copies(0, 0, 0):
            cp.start()

    @pl.when(s + 1 < pl.num_programs(0))
    def _():
        for cp in chunk_copies(s + 1, 0, 1 - slot):
            cp.start()

    qs[...] = q_ref[...].astype(F32).reshape(rows, LANES).astype(BF16)
    tpos, spos = _new_positions(rows, tn)
    z = _dot_nt(qs[...], _pad_new(kn_ref, tn))
    w, tot = _sb_tile(z, tri_ref[0:PAGE_SIZE, 0:PAGE_SIZE], jnp.zeros((rows, PAGE_SIZE), F32), spos < tpos)
    acc_ref[...] = _dot(w.astype(BF16), _pad_new(vn_ref, tn))
    run_ref[...] = jnp.broadcast_to(tot, (rows, LANES))

    def consume():
        q = qs[...]
        z = jnp.concatenate([_dot(q, kbuf[slot, i].astype(BF16)) for i in range(ppc)], axis=1)
        run = run_ref[...]
        w, tot = _sb_tile(z, tri_ref[...], _rep_lanes(run, tc), None)
        wb = w.astype(BF16)
        acc = acc_ref[...]
        for i in range(ppc):
            acc = acc + _dot_nt(_page_slice(wb, i), vbuf[slot, i].astype(BF16))
        acc_ref[...] = acc
        run = run + tot
        run_ref[...] = run
        return (jnp.min(run) < SB_DEAD_LOG2).astype(jnp.int32)

    for cp in chunk_copies(s, 0, slot):
        cp.wait()
    live0 = consume()

    def more(carry):
        chunk, live = carry
        return jnp.logical_and(chunk < nchunks, live == 1)

    def next_chunk(carry):
        chunk, _ = carry
        cps = chunk_copies(s, chunk, slot)
        for cp in cps:
            cp.start()
        for cp in cps:
            cp.wait()
        return chunk + 1, consume()

    lax.while_loop(more, next_chunk, (jnp.int32(1), live0))

    a = acc_ref[...]
    half = SB_GROUP * tn
    _sb_pair_store(o_ref, a[:half], a[half:], tn)


def _ssb_call(layer, page_table, qsb, kn, vn, cache_kt, cache_vt, tn):
    nseq, npages = page_table.shape
    ppc = min(4, npages)
    tc = ppc * PAGE_SIZE
    rows = SB_HEADS * tn
    tri = _tri_matrix(tc)
    page_shape = cache_kt.shape[2:]
    grid_spec = pltpu.PrefetchScalarGridSpec(
        num_scalar_prefetch=1, grid=(nseq,),
        in_specs=[
            pl.BlockSpec((SB_HEADS, tn, LANES), lambda s, pt: (0, s, 0)),
            pl.BlockSpec((tn, SB_KV_WIDTH), lambda s, pt: (s, 0)),
            pl.BlockSpec((tn, SB_KV_WIDTH), lambda s, pt: (s, 0)),
            pl.BlockSpec(tri.shape, lambda s, pt: (0, 0)),
            pl.BlockSpec(memory_space=pl.ANY),
            pl.BlockSpec(memory_space=pl.ANY),
        ],
        out_specs=pl.BlockSpec((tn, SB_WIDTH), lambda s, pt: (s, 0)),
        scratch_shapes=[
            pltpu.VMEM((2, ppc) + page_shape, F32), pltpu.VMEM((2, ppc) + page_shape, F32),
            pltpu.SemaphoreType.DMA((2, 2)),
            pltpu.VMEM((rows, LANES), BF16), pltpu.VMEM((rows, LANES), F32), pltpu.VMEM((rows, LANES), F32),
        ],
    )
    return pl.pallas_call(
        functools.partial(_ssb_kernel, layer=layer, npages=npages, ppc=ppc, tn=tn), grid_spec=grid_spec,
        out_shape=jax.ShapeDtypeStruct((nseq * tn, SB_WIDTH), F32),
        compiler_params=_cparams(("arbitrary",)), name="ssb",
    )(page_table, qsb, kn, vn, tri, cache_kt, cache_vt)


def _smla_kernel(pt_ref, ql_ref, qr_ref, cn_ref, krn_ref, wuv_ref, c_hbm, rt_hbm, o_ref, cbuf, rbuf, sem,
                 ql_s, qr_s, m_ref, l_ref, acc_m, *, layer, npages, npg, tn):
    s = pl.program_id(0)
    nseq = pl.num_programs(0)
    rows = MLA_HEADS * tn
    nchunks = npages // npg

    def chunk_copies(seq, chunk, slot):
        cps = []
        for i in range(npg):
            page = pt_ref[seq, npg * chunk + i]
            cps.append(pltpu.make_async_copy(c_hbm.at[layer, page], cbuf.at[slot, i], sem.at[slot, 0]))
            cps.append(pltpu.make_async_copy(rt_hbm.at[layer, page], rbuf.at[slot, i], sem.at[slot, 1]))
        return cps

    def start(seq, chunk, slot):
        for cp in chunk_copies(seq, chunk, slot):
            cp.start()

    def wait(seq, chunk, slot):
        for cp in chunk_copies(seq, chunk, slot):
            cp.wait()

    @pl.when(s == 0)
    def _():
        start(0, 0, 0)

    ql_s[...] = ql_ref[...].astype(F32).reshape(rows, MLA_KV_RANK).astype(BF16)
    qr_s[...] = qr_ref[...].astype(F32).reshape(rows, MLA_ROPE_DIM).astype(BF16)
    cnb = _pad_new(cn_ref, tn)
    tpos, spos = _new_positions(rows, tn)
    s0 = _dot_nt(ql_s[...], cnb) + _dot_nt(qr_s[...], _pad_new(krn_ref, tn))
    m_ref[...] = jnp.full_like(m_ref, NEG_BIG)
    l_ref[...] = jnp.zeros_like(l_ref)
    acc_m[...] = jnp.zeros_like(acc_m)
    _mla_update(jnp.where(spos <= tpos, s0, NEG_BIG), cnb, m_ref, l_ref, acc_m)

    def consume(slot):
        ql = ql_s[...]
        qr = qr_s[...]
        sc = jnp.concatenate([_dot_nt(ql, cbuf[slot, i].astype(BF16)) + _dot(qr, rbuf[slot, i].astype(BF16))
                              for i in range(npg)], axis=1)
        m_prev = m_ref[...]
        m_new = jnp.maximum(m_prev, jnp.max(sc, axis=1, keepdims=True))
        p = jnp.exp2((sc - _rep_lanes(m_new, npg * PAGE_SIZE)) * MLA_EXP2_SCALE)
        alpha = jnp.exp2((m_prev - m_new) * MLA_EXP2_SCALE)
        l_ref[...] = alpha * l_ref[...] + jnp.sum(p, axis=1, keepdims=True)
        pb = p.astype(BF16)
        pv = None
        for i in range(npg):
            t = _dot(_page_slice(pb, i), cbuf[slot, i].astype(BF16))
            pv = t if pv is None else pv + t
        acc_m[...] = _rep_lanes(alpha, MLA_KV_RANK) * acc_m[...] + pv
        m_ref[...] = m_new

    def pair(jj, carry):
        c0 = 2 * jj
        wait(s, c0, 0)
        start(s, c0 + 1, 1)
        consume(0)
        wait(s, c0 + 1, 1)

        @pl.when(c0 + 2 < nchunks)
        def _():
            start(s, c0 + 2, 0)

        @pl.when(jnp.logical_and(c0 + 2 >= nchunks, s + 1 < nseq))
        def _():
            start(s + 1, 0, 0)

        consume(1)
        return carry

    lax.fori_loop(0, nchunks // 2, pair, 0)
    _mla_finish(o_ref, wuv_ref, l_ref, acc_m, tn)


def _smla_call(layer, page_table, qlat, qrope, cn, krn, wuv, cache_c, cache_rt, tn):
    nseq, npages = page_table.shape
    npg = min(16, npages // 2)
    assert npages % (2 * npg) == 0
    rows = MLA_HEADS * tn
    grid_spec = pltpu.PrefetchScalarGridSpec(
        num_scalar_prefetch=1, grid=(nseq,),
        in_specs=[
            pl.BlockSpec((MLA_HEADS, tn, MLA_KV_RANK), lambda s, pt: (0, s, 0)),
            pl.BlockSpec((MLA_HEADS, tn, MLA_ROPE_DIM), lambda s, pt: (0, s, 0)),
            pl.BlockSpec((tn, MLA_KV_RANK), lambda s, pt: (s, 0)),
            pl.BlockSpec((tn, MLA_ROPE_DIM), lambda s, pt: (s, 0)),
            pl.BlockSpec(wuv.shape, lambda s, pt: (0, 0, 0)),
            pl.BlockSpec(memory_space=pl.ANY),
            pl.BlockSpec(memory_space=pl.ANY),
        ],
        out_specs=pl.BlockSpec((tn, MLA_WIDTH), lambda s, pt: (s, 0)),
        scratch_shapes=[
            pltpu.VMEM((2, npg) + cache_c.shape[2:], F32), pltpu.VMEM((2, npg) + cache_rt.shape[2:], F32),
            pltpu.SemaphoreType.DMA((2, 2)),
            pltpu.VMEM((rows, MLA_KV_RANK), BF16), pltpu.VMEM((rows, MLA_ROPE_DIM), BF16),
            pltpu.VMEM((rows, LANES), F32), pltpu.VMEM((rows, LANES), F32),
            pltpu.VMEM((rows, MLA_KV_RANK), F32),
        ],
    )
    return pl.pallas_call(
        functools.partial(_smla_kernel, layer=layer, npages=npages, npg=npg, tn=tn), grid_spec=grid_spec,
        out_shape=jax.ShapeDtypeStruct((nseq * tn, MLA_WIDTH), F32),
        compiler_params=_cparams(("arbitrary",)), name="smla",
    )(page_table, qlat, qrope, cn, krn, wuv, cache_c, cache_rt)


def _out_kernel(x_ref, brc_ref, osb_ref, omla_ref, ng_ref, wz_ref, wg_ref, woc_ref, wos_ref, wom_ref, wo_ref,
                fg_ref, o_ref, *, final):
    x = x_ref[...]
    ub = _rms(x, ng_ref[...]).astype(BF16)
    z = _dot(ub, wz_ref[...])
    merged = None
    branches = ((brc_ref, woc_ref, CONV_CH), (osb_ref, wos_ref, SB_WIDTH), (omla_ref, wom_ref, MLA_WIDTH))
    off = 0
    for i, (b_ref, w_ref, width) in enumerate(branches):
        br = (b_ref[...] * _silu(z[:, off:off + width])).astype(BF16)
        off += width
        gate = _sigmoid(_dot(ub, wg_ref[:, i * D_MODEL:(i + 1) * D_MODEL]))
        term = gate * _dot(br, w_ref[...])
        merged = term if merged is None else merged + term
    out = x + _dot(merged.astype(BF16), wo_ref[...])
    if final:
        out = _rms(out, fg_ref[...])
    o_ref[...] = out


def _out_call(x, brc, osb, omla, lw, final_g, final, tm):
    n = x.shape[0]
    row = lambda i: (i, 0)
    c2 = lambda i: (0, 0)
    wnames = ('wz', 'wg', 'woc', 'wos', 'wom', 'wo')
    in_specs = [pl.BlockSpec((tm, D_MODEL), row), pl.BlockSpec((tm, CONV_CH), row),
                pl.BlockSpec((tm, SB_WIDTH), row), pl.BlockSpec((tm, MLA_WIDTH), row),
                pl.BlockSpec((1, D_MODEL), c2)]
    in_specs += [pl.BlockSpec(lw[k].shape, c2) for k in wnames]
    in_specs += [pl.BlockSpec((1, D_MODEL), c2)]
    return pl.pallas_call(
        functools.partial(_out_kernel, final=final), grid=(n // tm,), in_specs=in_specs,
        out_specs=pl.BlockSpec((tm, D_MODEL), row), out_shape=jax.ShapeDtypeStruct((n, D_MODEL), F32),
        compiler_params=_cparams(("parallel",)), name="out",
    )(x, brc, osb, omla, lw['norm_g'], *[lw[k] for k in wnames], final_g)


def _layer_weights(l, norm_g, w_in, conv_w, conv_b, conv_ln_g, conv_ln_b, mla_q_norm_g, mla_w_uq, mla_kv_norm_g,
                   mla_w_uk, mla_w_uv, w_out_conv, w_out_sb, w_out_mla, w_o):
    wi = w_in[l]
    col = lambda i: wi[:, IN_OFF[i]:IN_OFF[i + 1]]
    half = MLA_ROPE_DIM // 2
    wq = col(3).reshape(D_MODEL, SB_HEADS, SB_HEAD_DIM)
    zq = jnp.zeros_like(wq)
    grp = (jnp.arange(SB_HEADS) // SB_GROUP)[None, :, None]
    wq_pad = jnp.where(grp == 0, jnp.concatenate([wq, zq], -1), jnp.concatenate([zq, wq], -1))
    wsb = jnp.concatenate([wq_pad.reshape(D_MODEL, SB_HEADS * LANES), col(4), col(5)], axis=1)
    wkr = col(9)
    wkr_sw = jnp.concatenate([-wkr[:, half:], wkr[:, :half]], axis=1)
    padk = jnp.zeros((D_MODEL, LANES - MLA_ROPE_DIM), F32)
    wmla = jnp.concatenate([col(7), col(8), wkr, padk, wkr_sw, padk], axis=1)
    wuq = mla_w_uq[l].reshape(MLA_Q_RANK, MLA_HEADS, MLA_NOPE_DIM + MLA_ROPE_DIM)
    wuq_n = wuq[:, :, :MLA_NOPE_DIM].reshape(MLA_Q_RANK, MLA_HEADS * MLA_NOPE_DIM)
    wuq_r = wuq[:, :, MLA_NOPE_DIM:]
    wuq_r1 = wuq_r.reshape(MLA_Q_RANK, MLA_HEADS * MLA_ROPE_DIM)
    wuq_r2 = jnp.concatenate([-wuq_r[:, :, half:], wuq_r[:, :, :half]], axis=-1).reshape(MLA_Q_RANK, -1)
    wuq_cat = jnp.concatenate([wuq_n, wuq_r1, wuq_r2], axis=1)
    ukt = jnp.transpose(mla_w_uk[l], (1, 2, 0))
    zk = jnp.zeros_like(ukt)
    odd = (jnp.arange(MLA_HEADS) % 2)[:, None, None]
    wuk = jnp.where(odd == 0, jnp.concatenate([ukt, zk], 1), jnp.concatenate([zk, ukt], 1))
    wuv2 = mla_w_uv[l].reshape(MLA_KV_RANK, MLA_WIDTH)
    hcol = (jnp.arange(MLA_WIDTH) // MLA_V_DIM)[None, None, :]
    wuv = jnp.where(hcol == jnp.arange(MLA_HEADS)[:, None, None], wuv2[None], 0.0)
    order = np.array(SB_HEAD_ORDER)
    wz_sb = col(6).reshape(D_MODEL, SB_HEADS, SB_HEAD_DIM)[:, order].reshape(D_MODEL, SB_WIDTH)
    wos = w_out_sb[l].reshape(SB_HEADS, SB_HEAD_DIM, D_MODEL)[order].reshape(SB_WIDTH, D_MODEL)
    bf = lambda a: a.astype(BF16)
    return {
        'norm_g': norm_g[l][None], 'wa': bf(jnp.concatenate([col(0), col(1)], axis=1)), 'wsb': bf(wsb),
        'wmla': bf(wmla), 'qg': mla_q_norm_g[l][None], 'wuq': bf(wuq_cat), 'kvg': mla_kv_norm_g[l][None],
        'wuk': bf(wuk), 'wuv': bf(wuv),
        'conv_w': conv_w[l], 'conv_b': conv_b[l][None], 'conv_ln_g': conv_ln_g[l][None],
        'conv_ln_b': conv_ln_b[l][None],
        'wz': bf(jnp.concatenate([col(2), wz_sb, col(10)], axis=1)), 'wg': bf(col(11)),
        'woc': bf(w_out_conv[l]), 'wos': bf(wos), 'wom': bf(w_out_mla[l]), 'wo': bf(w_o[l]),
    }


def _rope_tables(pos):
    inv = ROPE_THETA ** (-jnp.arange(0, MLA_ROPE_DIM, 2, dtype=F32) / MLA_ROPE_DIM)
    ang = pos.astype(F32)[:, None] * inv[None, :]
    cos = jnp.cos(ang)
    sin = jnp.sin(ang)
    return (jnp.tile(jnp.concatenate([cos, cos], axis=1), (1, MLA_HEADS)),
            jnp.tile(jnp.concatenate([sin, sin], axis=1), (1, MLA_HEADS)))


def kernel(x_prompt, x_sample, cache_sb_k, cache_sb_v, cache_mla_latent, cache_mla_rope, state_conv, page_table,
           norm_g, w_in, conv_w, conv_b, conv_ln_g, conv_ln_b, mla_q_norm_g, mla_w_uq, mla_kv_norm_g, mla_w_uk,
           mla_w_uv, w_out_conv, w_out_sb, w_out_mla, w_o, final_norm_g):
    b_p, seq, _ = x_prompt.shape
    b_s, t_new, _ = x_sample.shape
    depth = norm_g.shape[0]
    n_pool = cache_sb_k.shape[1]
    past_len = page_table.shape[1] * PAGE_SIZE
    n_p = b_p * seq
    n_s = b_s * t_new
    tm_p = min(256, n_p)
    tm_s = min(256, n_s)

    cos_p, sin_p = _rope_tables(jnp.arange(seq, dtype=jnp.int32))
    pos_s = past_len + (jnp.arange(tm_s, dtype=jnp.int32) % t_new)
    cos_s, sin_s = _rope_tables(pos_s)
    cache_kt = jnp.transpose(cache_sb_k, (0, 1, 3, 4, 2)).reshape(depth, n_pool, SB_KV_WIDTH, PAGE_SIZE)
    cache_vt = jnp.transpose(cache_sb_v, (0, 1, 3, 4, 2)).reshape(depth, n_pool, SB_KV_WIDTH, PAGE_SIZE)
    cache_rt = jnp.transpose(cache_mla_rope, (0, 1, 3, 2))
    fg = final_norm_g[None]
    zero_prefix = jnp.zeros((b_p, CONV_HALO, CONV_CH), F32)
    halo_pad = CONV_HALO - (CONV_WIDTH - 1)

    xp = x_prompt.reshape(n_p, D_MODEL)
    xs = x_sample.reshape(n_s, D_MODEL)
    st_p, st_s = [], []
    for l in range(depth):
        lw = _layer_weights(l, norm_g, w_in, conv_w, conv_b, conv_ln_g, conv_ln_b, mla_q_norm_g, mla_w_uq,
                            mla_kv_norm_g, mla_w_uk, mla_w_uv, w_out_conv, w_out_sb, w_out_mla, w_o)
        final = l == depth - 1
        h, qsb, k, v, qlat, qrope, ckv, kr, kb16, vb16, ckvb16, krb16 = _pre_call(xp, cos_p, sin_p, lw, tm_p)
        h3 = h.reshape(b_p, seq, CONV_CH)
        brc = _conv_call(h3, zero_prefix, lw).reshape(n_p, CONV_CH)
        osb = _sbp_call(qsb, kb16, vb16, b_p, seq)
        omla = _mlap_call(qlat, qrope, ckvb16, krb16, lw['wuv'], b_p, seq)
        xp = _out_call(xp, brc, osb, omla, lw, fg, final, tm_p)
        hist_p = jnp.concatenate([jnp.zeros((b_p, CONV_WIDTH - 1, CONV_CH), F32), h3], axis=1)
        st_p.append((k.reshape(b_p, seq, SB_KV_HEADS, SB_HEAD_DIM), v.reshape(b_p, seq, SB_KV_HEADS, SB_HEAD_DIM),
                     ckv.reshape(b_p, seq, MLA_KV_RANK), kr.reshape(b_p, seq, MLA_ROPE_DIM),
                     hist_p[:, -(CONV_WIDTH - 1):]))
        h, qsb, k, v, qlat, qrope, ckv, kr = _pre_call(xs, cos_s, sin_s, lw, tm_s)[:8]
        h3 = h.reshape(b_s, t_new, CONV_CH)
        prefix = jnp.pad(state_conv[l], ((0, 0), (halo_pad, 0), (0, 0)))
        brc = _conv_call(h3, prefix, lw).reshape(n_s, CONV_CH)
        osb = _ssb_call(l, page_table, qsb, k, v, cache_kt, cache_vt, t_new)
        omla = _smla_call(l, page_table, qlat, qrope, ckv, kr, lw['wuv'], cache_mla_latent, cache_rt, t_new)
        xs = _out_call(xs, brc, osb, omla, lw, fg, final, tm_s)
        hist_s = jnp.concatenate([state_conv[l], h3], axis=1)
        st_s.append((k.reshape(b_s, t_new, SB_KV_HEADS, SB_HEAD_DIM), v.reshape(b_s, t_new, SB_KV_HEADS, SB_HEAD_DIM),
                     ckv.reshape(b_s, t_new, MLA_KV_RANK), kr.reshape(b_s, t_new, MLA_ROPE_DIM),
                     hist_s[:, -(CONV_WIDTH - 1):]))

    outs = [xp.reshape(b_p, seq, D_MODEL), xs.reshape(b_s, t_new, D_MODEL)]
    for st in (st_p, st_s):
        for i in range(5):
            outs.append(jnp.stack([s[i] for s in st]))
    return tuple(outs)
```
